```python
import jax, jax.numpy as jnp
from jax import lax
import numpy as np

D_MODEL = 1024
BATCH = 8
SEQ = 2048
DEPTH = 4
DEC_BATCH = 16
DEC_SEQ = 64
PAST_LEN = 1024

CHUNK = 64
N_EVEN = (DEPTH + 1) // 2
N_ODD = DEPTH // 2
MIX_WIDTH = D_MODEL
HALF = MIX_WIDTH // 2
M_HEADS = 4
M_DV = HALF // M_HEADS
M_DK = M_DV // 2
G_HEADS = 4
G_DV = HALF // G_HEADS
G_DK = G_DV // 2
G_RANK = 16
G_TAU = 16.0
R_WIDTH = HALF
R_BLOCKS = 8
R_BS = R_WIDTH // R_BLOCKS
CONV_W = 4
LRU_C = 8.0
T_HEADS = 4
T_DV = HALF // T_HEADS
T_DK = T_DV // 2
ROPE_BASE = 10000.0
N_GROUPS = 4
EXPERTS_PER_GROUP = 4
N_EXPERTS = N_GROUPS * EXPERTS_PER_GROUP
TOP_K = 2
D_EXPERT = D_MODEL // 4
DN_ALPHA = (2 * DEPTH) ** 0.25
DN_BETA = (8 * DEPTH) ** -0.25
LN_EPS = 1e-5
HEAD_EPS = 1e-6
EVEN_SPLITS = (M_HEADS * M_DK, M_HEADS * M_DK, M_HEADS * M_DV, M_HEADS, M_HEADS, M_HEADS * M_DV,
               G_HEADS * G_DK, G_HEADS * G_DK, G_HEADS * G_DV, G_RANK, G_HEADS * G_DV)
ODD_SPLITS = (R_WIDTH, R_WIDTH, T_HEADS * T_DK, T_HEADS * T_DK, T_HEADS * T_DV, T_HEADS * T_DV)
EVEN_IN = sum(EVEN_SPLITS)
ODD_IN = sum(ODD_SPLITS)

kernel_name = 'hybrid_stream_mlstm_gla_rglru_retnet_hmoe'


def _split(z, sizes):
    return jnp.split(z, np.cumsum(sizes)[:-1].tolist(), axis=-1)


def _heads(a, h):
    return a.reshape(a.shape[:2] + (h, -1))


def _chunks(a, L):
    b, t = a.shape[:2]
    return a.reshape((b, t // L, L) + a.shape[2:]).swapaxes(0, 1)


def _unchunk(a):
    a = a.swapaxes(0, 1)
    return a.reshape((a.shape[0], a.shape[1] * a.shape[2]) + a.shape[3:])


def _layer_norm(x, g, b):
    xf = x.astype(jnp.float32)
    mu = xf.mean(-1, keepdims=True)
    var = jnp.square(xf - mu).mean(-1, keepdims=True)
    return ((xf - mu) * lax.rsqrt(var + LN_EPS) * g + b).astype(x.dtype)


def _head_norm(h, g):
    mu = h.mean(-1, keepdims=True)
    var = jnp.square(h - mu).mean(-1, keepdims=True)
    hn = (h - mu) * lax.rsqrt(var + HEAD_EPS)
    return hn.reshape(h.shape[:2] + (-1,)) * g


def _rotary(a, pos):
    d = a.shape[-1]
    inv = ROPE_BASE ** (-jnp.arange(0, d, 2, dtype=jnp.float32) / d)
    ang = pos[:, None] * inv[None]
    cos = jnp.cos(ang)[None, :, None]
    sin = jnp.sin(ang)[None, :, None]
    a1, a2 = a[..., : d // 2], a[..., d // 2:]
    return jnp.concatenate([a1 * cos - a2 * sin, a1 * sin + a2 * cos], axis=-1)


def _mlstm(q, k, v, i_pre, log_f, c0, n0, m0):
    L = min(CHUNK, q.shape[1])
    mask = jnp.tril(jnp.ones((L, L), bool))[None, :, :, None]

    def step(carry, inp):
        c, n, m = carry
        qc, kc, vc, ic, fc = inp
        b = jnp.cumsum(fc, axis=1)
        log_d = b[:, :, None] - b[:, None, :] + ic[:, None, :]
        log_d = jnp.where(mask, log_d, -jnp.inf)
        log_inter = b + m[:, None]
        m_row = jnp.maximum(log_inter, log_d.max(axis=2))
        w_intra = jnp.exp(log_d - m_row[:, :, None])
        w_inter = jnp.exp(log_inter - m_row)
        s = jnp.einsum('bjhk,bshk->bjsh', qc, kc) * w_intra
        num = jnp.einsum('bjsh,bshv->bjhv', s, vc) + w_inter[..., None] * jnp.einsum('bjhk,bhkv->bjhv', qc, c)
        den = s.sum(axis=2) + w_inter * jnp.einsum('bjhk,bhk->bjh', qc, n)
        h = num / jnp.maximum(jnp.abs(den), jnp.exp(-m_row))[..., None]
        b_last = b[:, -1]
        log_w = b_last[:, None] - b + ic
        m_new = jnp.maximum(b_last + m, log_w.max(axis=1))
        w_state = jnp.exp(log_w - m_new[:, None])
        decay = jnp.exp(b_last + m - m_new)
        c_new = decay[..., None, None] * c + jnp.einsum('bsh,bshk,bshv->bhkv', w_state, kc, vc)
        n_new = decay[..., None] * n + jnp.einsum('bsh,bshk->bhk', w_state, kc)
        return (c_new, n_new, m_new), h

    xs = tuple(_chunks(a, L) for a in (q, k, v, i_pre, log_f))
    (c, n, m), h = lax.scan(step, (c0, n0, m0), xs)
    return _unchunk(h), c, n, m


def _gla(q, k, v, log_a, s0):
    L = min(CHUNK, q.shape[1])
    mask = jnp.tril(jnp.ones((L, L), bool))[None, :, :, None, None]

    def step(s, inp):
        qc, kc, vc, ac = inp
        b = jnp.cumsum(ac, axis=1)
        rel = jnp.where(mask, b[:, :, None] - b[:, None, :], -jnp.inf)
        att = jnp.einsum('bjhk,bshk,bjshk->bjsh', qc, kc, jnp.exp(rel))
        o = jnp.einsum('bjsh,bshv->bjhv', att, vc) + jnp.einsum('bjhk,bhkv->bjhv', qc * jnp.exp(b), s)
        b_last = b[:, -1]
        s_new = jnp.exp(b_last)[..., None] * s + jnp.einsum('bshk,bshv->bhkv', kc * jnp.exp(b_last[:, None] - b), vc)
        return s_new, o

    xs = tuple(_chunks(a, L) for a in (q, k, v, log_a))
    s, o = lax.scan(step, s0, xs)
    return _unchunk(o), s


def _retention(q, k, v, s0):
    L = min(CHUNK, q.shape[1])
    lg = jnp.log1p(-jnp.exp2(-5.0 - jnp.arange(T_HEADS, dtype=jnp.float32)))
    jj = jnp.arange(L, dtype=jnp.float32)
    causal = jnp.tril(jnp.ones((L, L), bool))[:, :, None]
    dec = jnp.exp(jnp.where(causal, (jj[:, None] - jj[None, :])[:, :, None] * lg, -jnp.inf))
    w_inter = jnp.exp((jj + 1.0)[:, None] * lg)
    w_key = jnp.exp((L - 1.0 - jj)[:, None] * lg)
    s_decay = jnp.exp(L * lg)

    def step(s, inp):
        qc, kc, vc = inp
        att = jnp.einsum('bjhk,bshk->bjsh', qc, kc) * dec
        o = jnp.einsum('bjsh,bshv->bjhv', att, vc) + w_inter[:, :, None] * jnp.einsum('bjhk,bhkv->bjhv', qc, s)
        s_new = s_decay[:, None, None] * s + jnp.einsum('bshk,bshv->bhkv', kc * w_key[:, :, None], vc)
        return s_new, o

    xs = tuple(_chunks(a, L) for a in (q, k, v))
    s, o = lax.scan(step, s0, xs)
    return _unchunk(o), s


def _lru_combine(lhs, rhs):
    a1, b1 = lhs
    a2, b2 = rhs
    return a1 * a2, a2 * b1 + b2


def _rglru_branch(xb, conv_buf, h0, conv_w, conv_b, w_r, b_r, w_i, b_i, lam):
    bsz, t, w = xb.shape
    xpad = jnp.concatenate([conv_buf, xb], axis=1)
    xc = conv_b + xpad[:, 0:t] * conv_w[0]
    for j in range(1, CONV_W):
        xc = xc + xpad[:, j:j + t] * conv_w[j]
    new_buf = xpad[:, t:]
    xg = xc.reshape(bsz, t, R_BLOCKS, R_BS)
    r = jax.nn.sigmoid(jnp.einsum('btnc,ncd->btnd', xg, w_r).reshape(bsz, t, w) + b_r)
    i = jax.nn.sigmoid(jnp.einsum('btnc,ncd->btnd', xg, w_i).reshape(bsz, t, w) + b_i)
    log_a = -LRU_C * r * jax.nn.softplus(-lam)
    a = jnp.exp(log_a)
    u = jnp.sqrt(-jnp.expm1(2.0 * log_a)) * (i * xc)
    u = u.at[:, 0].add(a[:, 0] * h0)
    _, h = lax.associative_scan(_lru_combine, (a, u), axis=1)
    return h, new_buf, h[:, -1]


def _even_mixer(x, c0, n0, m0, s0, w_in, b_i, b_f, g_mnorm, w_a2, b_a, g_gnorm, w_out):
    z = (x @ w_in).astype(jnp.float32)
    mq, mk, mv, mi, mf, mo, gq, gk, gv, ga, gr = _split(z, EVEN_SPLITS)
    h_m, c, n, m = _mlstm(_heads(mq, M_HEADS), _heads(mk, M_HEADS) * (M_DK ** -0.5), _heads(mv, M_HEADS),
                          mi + b_i, jax.nn.log_sigmoid(mf + b_f), c0, n0, m0)
    y_m = jax.nn.sigmoid(mo) * _head_norm(h_m, g_mnorm)
    log_a = jax.nn.log_sigmoid(ga @ w_a2 + b_a) / G_TAU
    h_g, s = _gla(_heads(gq, G_HEADS) * (G_DK ** -0.5), _heads(gk, G_HEADS), _heads(gv, G_HEADS),
                  _heads(log_a, G_HEADS), s0)
    y_g = jax.nn.silu(gr) * _head_norm(h_g, g_gnorm)
    y = jnp.concatenate([y_m, y_g], axis=-1).astype(x.dtype) @ w_out
    return y, c, n, m, s


def _odd_mixer(x, pos, conv_buf, h0, s0, w_in, conv_w, conv_b, w_r, b_r, w_i, b_i, lam, g_tnorm, w_out):
    z = (x @ w_in).astype(jnp.float32)
    xb, gb, tq, tk, tv, tg = _split(z, ODD_SPLITS)
    h_r, buf, h_last = _rglru_branch(xb, conv_buf, h0, conv_w, conv_b, w_r, b_r, w_i, b_i, lam)
    y_r = jax.nn.gelu(gb) * h_r
    q = _rotary(_heads(tq, T_HEADS), pos)
    k = _rotary(_heads(tk, T_HEADS), pos) * (T_DK ** -0.5)
    h_t, s = _retention(q, k, _heads(tv, T_HEADS), s0)
    y_t = jax.nn.silu(tg) * _head_norm(h_t, g_tnorm)
    y = jnp.concatenate([y_r, y_t], axis=-1).astype(x.dtype) @ w_out
    return y, buf, h_last, s


def _hier_moe(x, w_grp, b_grp, w_rt, b_rt, w_gate, w_up, w_down):
    bsz, t, d = x.shape
    xf = x.reshape(bsz * t, d)
    p_grp = jax.nn.softmax((xf @ w_grp).astype(jnp.float32) + b_grp, axis=-1)
    g_p, g_idx = lax.top_k(p_grp, 1)
    logits = ((xf @ w_rt).astype(jnp.float32) + b_rt).reshape(-1, N_GROUPS, EXPERTS_PER_GROUP)
    in_grp = jnp.einsum('ng,nge->ne', jax.nn.one_hot(g_idx[:, 0], N_GROUPS, dtype=jnp.float32), logits)
    e_logit, e_loc = lax.top_k(in_grp, TOP_K)
    e_w = jax.nn.softmax(e_logit, axis=-1) * g_p
    e_id = g_idx * EXPERTS_PER_GROUP + e_loc
    gates = jnp.sum(jax.nn.one_hot(e_id, N_EXPERTS, dtype=jnp.float32) * e_w[..., None], axis=1)
    h = jax.nn.silu(jnp.einsum('nd,edf->nef', xf, w_gate)) * jnp.einsum('nd,edf->nef', xf, w_up)
    y = jnp.einsum('nef,efd->nd', h * gates[..., None].astype(h.dtype), w_down)
    return y.reshape(bsz, t, d).astype(x.dtype)


def _trunk(x, pos0, states, weights):
    mc, mn, mm, gs, cb, rh, rs = states
    (w_in_even, b_mlstm_i, b_mlstm_f, g_mlstm_norm, w_gla_a2, b_gla_a, g_gla_norm, w_out_even,
     w_in_odd, w_conv, b_conv, w_lru_r, b_lru_r, w_lru_i, b_lru_i, lru_lambda, g_ret_norm, w_out_odd,
     ln1_g, ln1_b, ln2_g, ln2_b, w_router_group, b_router_group, w_router_expert, b_router_expert,
     w_exp_gate, w_exp_up, w_exp_down) = weights
    pos = pos0 + jnp.arange(x.shape[1], dtype=jnp.float32)
    new_c, new_n, new_m, new_g, new_conv, new_h, new_r = [], [], [], [], [], [], []
    for layer in range(DEPTH):
        li = layer // 2
        if layer % 2 == 0:
            y, c, n, m, s = _even_mixer(x, mc[li], mn[li], mm[li], gs[li], w_in_even[li], b_mlstm_i[li], b_mlstm_f[li],
                                        g_mlstm_norm[li], w_gla_a2[li], b_gla_a[li], g_gla_norm[li], w_out_even[li])
            new_c.append(c)
            new_n.append(n)
            new_m.append(m)
            new_g.append(s)
        else:
            y, buf, h_last, s = _odd_mixer(x, pos, cb[li], rh[li], rs[li], w_in_odd[li], w_conv[li], b_conv[li],
                                           w_lru_r[li], b_lru_r[li], w_lru_i[li], b_lru_i[li], lru_lambda[li],
                                           g_ret_norm[li], w_out_odd[li])
            new_conv.append(buf)
            new_h.append(h_last)
            new_r.append(s)
        x = _layer_norm(DN_ALPHA * x + y, ln1_g[layer], ln1_b[layer])
        moe = _hier_moe(x, w_router_group[layer], b_router_group[layer], w_router_expert[layer],
                        b_router_expert[layer], w_exp_gate[layer], w_exp_up[layer], w_exp_down[layer])
        x = _layer_norm(DN_ALPHA * x + moe, ln2_g[layer], ln2_b[layer])
    return x, (jnp.stack(new_c), jnp.stack(new_n), jnp.stack(new_m), jnp.stack(new_g),
               jnp.stack(new_conv), jnp.stack(new_h), jnp.stack(new_r))


def setup_inputs(seed: int = 0) -> dict:
    key = jax.random.key(seed)
    ks = iter(jax.random.split(key, 48))
    f32 = jnp.float32

    def nrm(shape, scale):
        return jax.random.normal(next(ks), shape, f32) * scale

    def gain(shape):
        return 1.0 + nrm(shape, 0.02)

    u = jax.random.uniform(next(ks), (N_ODD, R_WIDTH), f32, 0.9, 0.999)
    sig = u ** (1.0 / LRU_C)
    inp = {}
    inp['x_prompt'] = nrm((BATCH, SEQ, D_MODEL), 1.0)
    inp['x_sample'] = nrm((DEC_BATCH, DEC_SEQ, D_MODEL), 1.0)
    inp['state_mlstm_c'] = nrm((N_EVEN, DEC_BATCH, M_HEADS, M_DK, M_DV), 0.1)
    inp['state_mlstm_n'] = jnp.abs(nrm((N_EVEN, DEC_BATCH, M_HEADS, M_DK), 0.1))
    inp['state_mlstm_m'] = nrm((N_EVEN, DEC_BATCH, M_HEADS), 1.0)
    inp['state_gla'] = nrm((N_EVEN, DEC_BATCH, G_HEADS, G_DK, G_DV), 0.3)
    inp['state_conv'] = nrm((N_ODD, DEC_BATCH, CONV_W - 1, R_WIDTH), 1.0)
    inp['state_rglru'] = nrm((N_ODD, DEC_BATCH, R_WIDTH), 0.5)
    inp['state_ret'] = nrm((N_ODD, DEC_BATCH, T_HEADS, T_DK, T_DV), 0.3)
    inp['w_in_even'] = nrm((N_EVEN, D_MODEL, EVEN_IN), D_MODEL ** -0.5)
    inp['b_mlstm_i'] = nrm((N_EVEN, M_HEADS), 0.1)
    inp['b_mlstm_f'] = jnp.linspace(3.0, 6.0, M_HEADS, dtype=f32)[None] + nrm((N_EVEN, M_HEADS), 0.1)
    inp['g_mlstm_norm'] = gain((N_EVEN, HALF))
    inp['w_gla_a2'] = nrm((N_EVEN, G_RANK, G_HEADS * G_DK), G_RANK ** -0.5)
    inp['b_gla_a'] = nrm((N_EVEN, G_HEADS * G_DK), 0.1)
    inp['g_gla_norm'] = gain((N_EVEN, HALF))
    inp['w_out_even'] = nrm((N_EVEN, MIX_WIDTH, D_MODEL), MIX_WIDTH ** -0.5 * DN_BETA)
    inp['w_in_odd'] = nrm((N_ODD, D_MODEL, ODD_IN), D_MODEL ** -0.5)
    inp['w_conv'] = nrm((N_ODD, CONV_W, R_WIDTH), CONV_W ** -0.5)
    inp['b_conv'] = nrm((N_ODD, R_WIDTH), 0.02)
    inp['w_lru_r'] = nrm((N_ODD, R_BLOCKS, R_BS, R_BS), R_BS ** -0.5)
    inp['b_lru_r'] = nrm((N_ODD, R_WIDTH), 0.02)
    inp['w_lru_i'] = nrm((N_ODD, R_BLOCKS, R_BS, R_BS), R_BS ** -0.5)
    inp['b_lru_i'] = nrm((N_ODD, R_WIDTH), 0.02)
    inp['lru_lambda'] = jnp.log(sig) - jnp.log1p(-sig)
    inp['g_ret_norm'] = gain((N_ODD, HALF))
    inp['w_out_odd'] = nrm((N_ODD, MIX_WIDTH, D_MODEL), MIX_WIDTH ** -0.5 * DN_BETA)
    inp['ln1_g'] = gain((DEPTH, D_MODEL))
    inp['ln1_b'] = nrm((DEPTH, D_MODEL), 0.02)
    inp['ln2_g'] = gain((DEPTH, D_MODEL))
    inp['ln2_b'] = nrm((DEPTH, D_MODEL), 0.02)
    inp['w_router_group'] = nrm((DEPTH, D_MODEL, N_GROUPS), D_MODEL ** -0.5)
    inp['b_router_group'] = nrm((DEPTH, N_GROUPS), 0.01)
    inp['w_router_expert'] = nrm((DEPTH, D_MODEL, N_EXPERTS), D_MODEL ** -0.5)
    inp['b_router_expert'] = nrm((DEPTH, N_EXPERTS), 0.01)
    inp['w_exp_gate'] = nrm((DEPTH, N_EXPERTS, D_MODEL, D_EXPERT), D_MODEL ** -0.5)
    inp['w_exp_up'] = nrm((DEPTH, N_EXPERTS, D_MODEL, D_EXPERT), D_MODEL ** -0.5)
    inp['w_exp_down'] = nrm((DEPTH, N_EXPERTS, D_EXPERT, D_MODEL), D_EXPERT ** -0.5 * DN_BETA)
    return inp


def reference(x_prompt, x_sample, state_mlstm_c, state_mlstm_n, state_mlstm_m, state_gla, state_conv,
              state_rglru, state_ret, w_in_even, b_mlstm_i, b_mlstm_f, g_mlstm_norm, w_gla_a2, b_gla_a,
              g_gla_norm, w_out_even, w_in_odd, w_conv, b_conv, w_lru_r, b_lru_r, w_lru_i, b_lru_i,
              lru_lambda, g_ret_norm, w_out_odd, ln1_g, ln1_b, ln2_g, ln2_b, w_router_group, b_router_group,
              w_router_expert, b_router_expert, w_exp_gate, w_exp_up, w_exp_down):
    f32 = jnp.float32
    weights = (w_in_even, b_mlstm_i, b_mlstm_f, g_mlstm_norm, w_gla_a2, b_gla_a, g_gla_norm, w_out_even,
               w_in_odd, w_conv, b_conv, w_lru_r, b_lru_r, w_lru_i, b_lru_i, lru_lambda, g_ret_norm, w_out_odd,
               ln1_g, ln1_b, ln2_g, ln2_b, w_router_group, b_router_group, w_router_expert, b_router_expert,
               w_exp_gate, w_exp_up, w_exp_down)
    bp = x_prompt.shape[0]
    init_prompt = (jnp.zeros((N_EVEN, bp, M_HEADS, M_DK, M_DV), f32),
                   jnp.zeros((N_EVEN, bp, M_HEADS, M_DK), f32),
                   jnp.zeros((N_EVEN, bp, M_HEADS), f32),
                   jnp.zeros((N_EVEN, bp, G_HEADS, G_DK, G_DV), f32),
                   jnp.zeros((N_ODD, bp, CONV_W - 1, R_WIDTH), f32),
                   jnp.zeros((N_ODD, bp, R_WIDTH), f32),
                   jnp.zeros((N_ODD, bp, T_HEADS, T_DK, T_DV), f32))
    init_sample = (state_mlstm_c.astype(f32), state_mlstm_n.astype(f32), state_mlstm_m.astype(f32),
                   state_gla.astype(f32), state_conv.astype(f32), state_rglru.astype(f32), state_ret.astype(f32))
    y_prompt, (pc, pn, pm, pg, pconv, ph, pr) = _trunk(x_prompt, 0, init_prompt, weights)
    y_sample, (sc, sn, sm, sg, sconv, sh, sr) = _trunk(x_sample, PAST_LEN, init_sample, weights)
    return (y_prompt, y_sample,
            pc.astype(state_mlstm_c.dtype), pn.astype(state_mlstm_n.dtype), pm.astype(state_mlstm_m.dtype),
            pg.astype(state_gla.dtype), pconv.astype(state_conv.dtype), ph.astype(state_rglru.dtype),
            pr.astype(state_ret.dtype),
            sc.astype(state_mlstm_c.dtype), sn.astype(state_mlstm_n.dtype), sm.astype(state_mlstm_m.dtype),
            sg.astype(state_gla.dtype), sconv.astype(state_conv.dtype), sh.astype(state_rglru.dtype),
            sr.astype(state_ret.dtype))
```

```python
import functools
import math

import numpy as np
import jax
import jax.numpy as jnp
from jax import lax
from jax.experimental import pallas as pl
from jax.experimental.pallas import tpu as pltpu

F32 = jnp.float32
BF16 = jnp.bfloat16

D_MODEL = 1024
BATCH = 8
SEQ = 2048
DEPTH = 4
DEC_BATCH = 16
DEC_SEQ = 64
PAST_LEN = 1024
CHUNK = 64
HALF = D_MODEL // 2
N_HEADS = 4
DK = 64
DV = 128
G_RANK = 16
G_TAU = 16.0
R_WIDTH = HALF
R_BS = 64
CONV_W = 4
LRU_C = 8.0
ROPE_BASE = 10000.0
N_GROUPS = 4
EXPERTS_PER_GROUP = 4
N_EXPERTS = 16
D_EXPERT = D_MODEL // 4
DN_ALPHA = (2 * DEPTH) ** 0.25
LN_EPS = 1e-5
HEAD_EPS = 1e-6

SEQ_BLK = 8
ROW_BLK = SEQ_BLK * CHUNK
N_CHUNK = SEQ // CHUNK
N_SAMPLE_BLK = DEC_BATCH // SEQ_BLK
N_STEPS = N_CHUNK + N_SAMPLE_BLK
N_TOK = N_STEPS * ROW_BLK
N_STATE_SEQ = BATCH + DEC_BATCH

LANE = 128
EVEN_W = 3200
ODD_W = 2560
E_MQ, E_MK, E_MV, E_MO, E_GQ, E_GK, E_GV, E_GR, E_SM = 0, 256, 512, 1024, 1536, 1792, 2048, 2560, 3072
O_XB, O_GB, O_TQ, O_TK, O_TV, O_TG = 0, 512, 1024, 1280, 1536, 2048
GLA_LEVELS = (32, 16, 8, 4, 2, 1)
CM_ROWS = CHUNK * (1 + 2 * len(GLA_LEVELS))
ROUTE_E0 = 16
PROJ_TILE = 640
VMEM_LIMIT = 56 * 1024 * 1024


def _logsig(x):
    return jnp.minimum(x, 0.0) - jnp.log1p(jnp.exp(-jnp.abs(x)))


def _softplus(x):
    return jnp.maximum(x, 0.0) + jnp.log1p(jnp.exp(-jnp.abs(x)))


def _sigmoid(x):
    return 1.0 / (1.0 + jnp.exp(-x))


def _silu(x):
    return x * _sigmoid(x)


def _gelu_tanh(x):
    return 0.5 * x * (1.0 + jnp.tanh(math.sqrt(2.0 / math.pi) * (x + 0.044715 * (x * x * x))))


def _dot(a, b):
    return jnp.dot(a, b, preferred_element_type=F32)


def _dot_nt(a, b):
    return lax.dot_general(a, b, (((1,), (1,)), ((), ())), preferred_element_type=F32)


def _dot_tn(a, b):
    return lax.dot_general(a, b, (((0,), (0,)), ((), ())), preferred_element_type=F32)


def _split3(x):
    hi = x.astype(BF16)
    r1 = x - hi.astype(F32)
    mid = r1.astype(BF16)
    lo = (r1 - mid.astype(F32)).astype(BF16)
    return hi, mid, lo


def _rows_dot(mat01, x):
    hi, mid, lo = _split3(x)
    return _dot(mat01, hi) + _dot(mat01, mid) + _dot(mat01, lo)


def _head_norm(h, g):
    mu = jnp.mean(h, axis=-1, keepdims=True)
    d = h - mu
    var = jnp.mean(d * d, axis=-1, keepdims=True)
    return d * lax.rsqrt(var + HEAD_EPS) * g


def _layer_norm(x, g, b):
    mu = jnp.mean(x, axis=-1, keepdims=True)
    d = x - mu
    var = jnp.mean(d * d, axis=-1, keepdims=True)
    return d * lax.rsqrt(var + LN_EPS) * g + b


def _iota(shape, dim):
    return lax.broadcasted_iota(jnp.int32, shape, dim)


def _project_in(xb_ref, win_ref, z_scr, width):
    for c0 in range(0, width, PROJ_TILE):
        z_scr[:, c0:c0 + PROJ_TILE] = _dot(xb_ref[...], win_ref[:, c0:c0 + PROJ_TILE])


def _route(x1, wr_ref, br_ref):
    xh = x1.astype(BF16)
    xl = (x1 - xh.astype(F32)).astype(BF16)
    wh = wr_ref[:, 0:LANE]
    wl = wr_ref[:, LANE:2 * LANE]
    logits = _dot(xh, wh) + _dot(xh, wl) + _dot(xl, wh) + br_ref[...]
    lane = _iota(logits.shape, 1).astype(F32)
    neg = -jnp.inf
    big = float(4 * LANE)
    gl = jnp.where(lane < N_GROUPS, logits, neg)
    gmax = jnp.max(gl, axis=1, keepdims=True)
    gsum = jnp.sum(jnp.exp(gl - gmax), axis=1, keepdims=True)
    g_p = 1.0 / gsum
    g_idx = jnp.min(jnp.where(gl == gmax, lane, big), axis=1, keepdims=True)
    lo = ROUTE_E0 + EXPERTS_PER_GROUP * g_idx
    in_grp = (lane >= lo) & (lane < lo + EXPERTS_PER_GROUP)
    el = jnp.where(in_grp, logits, neg)
    e1 = jnp.max(el, axis=1, keepdims=True)
    i1 = jnp.min(jnp.where(el == e1, lane, big), axis=1, keepdims=True)
    el2 = jnp.where(lane == i1, neg, el)
    e2 = jnp.max(el2, axis=1, keepdims=True)
    i2 = jnp.min(jnp.where(el2 == e2, lane, big), axis=1, keepdims=True)
    t = jnp.exp(e2 - e1)
    w1 = g_p / (1.0 + t)
    w2 = g_p * t / (1.0 + t)
    return jnp.where(lane == i1, w1, 0.0) + jnp.where(lane == i2, w2, 0.0)


def _mixer_tail(x_ref, y_scr, wout_ref, ln_ref, wr_ref, br_ref, x1_ref, x1b_ref, gates_ref):
    y = _dot(y_scr[...].astype(BF16), wout_ref[...])
    x1 = _layer_norm(DN_ALPHA * x_ref[...] + y, ln_ref[0:1, :], ln_ref[1:2, :])
    x1_ref[...] = x1
    x1b_ref[...] = x1.astype(BF16)
    gates_ref[...] = _route(x1, wr_ref, br_ref)


def _init_states(j, pairs):
    @pl.when(j == 0)
    def _():
        for _, out_ref in pairs:
            out_ref[...] = jnp.zeros(out_ref.shape, out_ref.dtype)

    @pl.when(j >= N_CHUNK)
    def _():
        for in_ref, out_ref in pairs:
            out_ref[...] = in_ref[...]


def _even_kernel(x_ref, xb_ref, win_ref, wout_ref, ln_ref, wr_ref, br_ref, cm_ref, bsm_ref, gm_ref,
                 wa2_ref, ba_ref, gg_ref, c_in, n_in, m_in, s_in,
                 x1_ref, x1b_ref, gates_ref, c_out, n_out, m_out, s_out, z_scr, y_scr):
    j = pl.program_id(0)
    _init_states(j, ((c_in, c_out), (n_in, n_out), (m_in, m_out), (s_in, s_out)))
    _project_in(xb_ref, win_ref, z_scr, EVEN_W)

    rr = _iota((CHUNK, CHUNK), 0)
    cc = _iota((CHUNK, CHUNK), 1)
    causal = cc <= rr
    eye = cc == rr
    trow = _iota((CHUNK, 1), 0)
    tri = cm_ref[0:CHUNK, :]
    neg = -jnp.inf

    def seq_body(s, carry):
        rows = pl.ds(pl.multiple_of(s * CHUNK, CHUNK), CHUNK)
        n_all = n_out[s]
        m_all = m_out[s]
        lane_m = _iota((1, LANE), 1)
        n_new = []
        m_new = m_all
        slab = z_scr[rows, E_SM:E_SM + LANE]
        pre = slab + bsm_ref[...]
        fcum = _rows_dot(tri, _logsig(pre))
        at = jnp.concatenate([pre, fcum], axis=0).T

        for h in range(N_HEADS):
            bc = fcum[:, N_HEADS + h:N_HEADS + h + 1]
            icol = pre[:, h:h + 1]
            br = at[N_HEADS + h:N_HEADS + h + 1, CHUNK:2 * CHUNK]
            ir = at[h:h + 1, 0:CHUNK]
            mprev = m_all[:, h:h + 1]
            logd = jnp.where(causal, bc - br + ir, neg)
            linter = bc + mprev
            mrow = jnp.maximum(linter, jnp.max(logd, axis=1, keepdims=True))
            wintra = jnp.exp(logd - mrow)
            winter = jnp.exp(linter - mrow)
            qh = z_scr[rows, E_MQ + h * DK:E_MQ + (h + 1) * DK]
            kh = z_scr[rows, E_MK + h * DK:E_MK + (h + 1) * DK] * (DK ** -0.5)
            vb = z_scr[rows, E_MV + h * DV:E_MV + (h + 1) * DV].astype(BF16)
            qb = qh.astype(BF16)
            sm = _dot_nt(qb, kh.astype(BF16)) * wintra
            cst = c_out[s, h]
            nrow = n_all[:, h * DK:(h + 1) * DK]
            num = _dot(sm.astype(BF16), vb) + winter * _dot(qb, cst.astype(BF16))
            den = jnp.sum(sm, axis=1, keepdims=True) + winter * jnp.sum(qh * nrow, axis=1, keepdims=True)
            hh = num / jnp.maximum(jnp.abs(den), jnp.exp(-mrow))
            blast = bc[CHUNK - 1:CHUNK, :]
            logw = blast - bc + icol
            mnew = jnp.maximum(blast + mprev, jnp.max(logw, axis=0, keepdims=True))
            kw = kh * jnp.exp(logw - mnew)
            decay = jnp.exp(blast + mprev - mnew)
            c_out[s, h] = decay * cst + _dot_tn(kw.astype(BF16), vb)
            n_new.append(decay * nrow + jnp.sum(kw, axis=0, keepdims=True))
            m_new = jnp.where(lane_m == h, mnew, m_new)
            mo = z_scr[rows, E_MO + h * DV:E_MO + (h + 1) * DV]
            y_scr[rows, h * DV:(h + 1) * DV] = _sigmoid(mo) * _head_norm(hh, gm_ref[:, h * DV:(h + 1) * DV])

        n_out[s] = jnp.concatenate(n_new, axis=1)
        m_out[s] = m_new

        la =_logsig(_dot(slab.astype(BF16), wa2_ref[...]) + ba_ref[...]) * (1.0 / G_TAU)
        ex = _rows_dot(cm_ref[...], la)
        bg = ex[0:CHUNK]
        gq = z_scr[rows, E_GQ:E_GQ + N_HEADS * DK] * (DK ** -0.5)
        gk = z_scr[rows, E_GK:E_GK + N_HEADS * DK]
        blast = bg[CHUNK - 1:CHUNK, :]
        qdec = (gq * jnp.exp(bg)).astype(BF16)
        kdec = (gk * jnp.exp(blast - bg)).astype(BF16)
        lat = jnp.concatenate([la, jnp.zeros_like(la)], axis=0).T
        sdec = jnp.exp(jnp.sum(lat, axis=1, keepdims=True))
        gqb = gq.astype(BF16)
        gkb = gk.astype(BF16)
        qts, kts = [], []
        for li, m in enumerate(GLA_LEVELS):
            r0 = CHUNK * (1 + 2 * li)
            up = (trow & (2 * m - 1)) >= m
            qts.append(jnp.where(up, gq * jnp.exp(ex[r0:r0 + CHUNK]), 0.0).astype(BF16))
            kts.append(jnp.where(up, 0.0, gk * jnp.exp(ex[r0 + CHUNK:r0 + 2 * CHUNK])).astype(BF16))
        for h in range(N_HEADS):
            hs = slice(h * DK, (h + 1) * DK)
            att = jnp.where(eye, _dot_nt(gqb[:, hs], gkb[:, hs]), 0.0)
            for li, m in enumerate(GLA_LEVELS):
                sh = int(math.log2(2 * m))
                same = (rr >> sh) == (cc >> sh)
                att = att + jnp.where(same, _dot_nt(qts[li][:, hs], kts[li][:, hs]), 0.0)
            vb = z_scr[rows, E_GV + h * DV:E_GV + (h + 1) * DV].astype(BF16)
            sst = s_out[s, h]
            og = _dot(att.astype(BF16), vb) + _dot(qdec[:, hs], sst.astype(BF16))
            s_out[s, h] = sdec[h * DK:(h + 1) * DK, :] * sst + _dot_tn(kdec[:, hs], vb)
            gr = z_scr[rows, E_GR + h * DV:E_GR + (h + 1) * DV]
            y_scr[rows, HALF + h * DV:HALF + (h + 1) * DV] = _silu(gr) * _head_norm(og, gg_ref[:, h * DV:(h + 1) * DV])
        return carry

    lax.fori_loop(0, SEQ_BLK, seq_body, 0)
    _mixer_tail(x_ref, y_scr, wout_ref, ln_ref, wr_ref, br_ref, x1_ref, x1b_ref, gates_ref)


def _odd_kernel(x_ref, xb_ref, win_ref, wout_ref, ln_ref, wr_ref, br_ref, rope_ref, cw_ref, wri_ref, bri_ref,
                lam_ref, gt_ref, cv_in, h_in, r_in,
                x1_ref, x1b_ref, gates_ref, cv_out, h_out, r_out, z_scr, y_scr, pad_scr):
    j = pl.program_id(0)
    _init_states(j, ((cv_in, cv_out), (h_in, h_out), (r_in, r_out)))
    _project_in(xb_ref, win_ref, z_scr, ODD_W)

    rr = _iota((CHUNK, CHUNK), 0)
    cc = _iota((CHUNK, CHUNK), 1)
    causal = cc <= rr
    dist = (rr - cc).astype(F32)
    trow = _iota((CHUNK, 1), 0)
    tf = trow.astype(F32)
    lane4 = _iota((CHUNK, N_HEADS * DK), 1)
    first_half = (lane4 & (DK - 1)) < (DK // 2)
    lgs = [math.log1p(-(2.0 ** (-5.0 - h))) for h in range(N_HEADS)]
    sp = _softplus(-lam_ref[...])
    cosf = rope_ref[:, 0:N_HEADS * DK]
    sinf = rope_ref[:, N_HEADS * DK:2 * N_HEADS * DK]

    def rot(a):
        swapped = jnp.where(first_half, pltpu.roll(a, N_HEADS * DK - DK // 2, 1), pltpu.roll(a, DK // 2, 1))
        return a * cosf + swapped * sinf

    def seq_body(s, carry):
        rows = pl.ds(pl.multiple_of(s * CHUNK, CHUNK), CHUNK)
        xb = z_scr[rows, O_XB:O_XB + R_WIDTH]
        pad_scr[8 - (CONV_W - 1):8, :] = cv_out[s]
        pad_scr[8:8 + CHUNK, :] = xb
        xc = cw_ref[CONV_W:CONV_W + 1, :] + xb * cw_ref[CONV_W - 1:CONV_W, :]
        for d in range(1, CONV_W):
            xc = xc + pad_scr[8 - d:8 - d + CHUNK, :] * cw_ref[CONV_W - 1 - d:CONV_W - d, :]
        cv_out[s] = xb[CHUNK - (CONV_W - 1):CHUNK, :]
        xcb = xc.astype(BF16)
        halves = []
        for t in range(2):
            halves.append(_dot(xcb[:, t * 256:(t + 1) * 256], wri_ref[t]))
        r_pre = jnp.concatenate([halves[0][:, 0:256], halves[1][:, 0:256]], axis=1) + bri_ref[0:1, :]
        i_pre = jnp.concatenate([halves[0][:, 256:512], halves[1][:, 256:512]], axis=1) + bri_ref[1:2, :]
        log_a = (-LRU_C) * _sigmoid(r_pre) * sp
        a = jnp.exp(log_a)
        th = jnp.tanh(log_a)
        u = jnp.sqrt((-2.0 * th) / (1.0 - th)) * (_sigmoid(i_pre) * xc)
        u = u + jnp.where(trow == 0, a * h_out[s], 0.0)
        for sh in (1, 2, 4, 8, 16, 32):
            valid = trow >= sh
            u_prev = pltpu.roll(u, sh, 0)
            a_prev = pltpu.roll(a, sh, 0)
            u = jnp.where(valid, a * u_prev + u, u)
            a = jnp.where(valid, a * a_prev, a)
        h_out[s] = u[CHUNK - 1:CHUNK, :]
        y_scr[rows, 0:R_WIDTH] = _gelu_tanh(z_scr[rows, O_GB:O_GB + R_WIDTH]) * u

        q = rot(z_scr[rows, O_TQ:O_TQ + N_HEADS * DK]).astype(BF16)
        kf = rot(z_scr[rows, O_TK:O_TK + N_HEADS * DK]) * (DK ** -0.5)
        for h in range(N_HEADS):
            hs = slice(h * DK, (h + 1) * DK)
            lg = lgs[h]
            dec = jnp.where(causal, jnp.exp(dist * lg), 0.0)
            w_inter = jnp.exp((tf + 1.0) * lg)
            w_key = jnp.exp((CHUNK - 1.0 - tf) * lg)
            s_decay = math.exp(CHUNK * lg)
            qh = q[:, hs]
            kh = kf[:, hs]
            vb = z_scr[rows, O_TV + h * DV:O_TV + (h + 1) * DV].astype(BF16)
            sst = r_out[s, h]
            att = _dot_nt(qh, kh.astype(BF16)) * dec
            o = _dot(att.astype(BF16), vb) + w_inter * _dot(qh, sst.astype(BF16))
            r_out[s, h] = s_decay * sst + _dot_tn((kh * w_key).astype(BF16), vb)
            tg = z_scr[rows, O_TG + h * DV:O_TG + (h + 1) * DV]
            y_scr[rows, HALF + h * DV:HALF + (h + 1) * DV] = _silu(tg) * _head_norm(o, gt_ref[:, h * DV:(h + 1) * DV])
        return carry

    lax.fori_loop(0, SEQ_BLK, seq_body, 0)
    _mixer_tail(x_ref, y_scr, wout_ref, ln_ref, wr_ref, br_ref, x1_ref, x1b_ref, gates_ref)


def _moe_kernel(x1_ref, x1b_ref, gates_ref, wg_ref, wu_ref, wd_ref, ln_ref, x2_ref, x2b_ref, acc_scr):
    e = pl.program_id(1)

    @pl.when(e == 0)
    def _():
        acc_scr[...] = jnp.zeros(acc_scr.shape, F32)

    xb = x1b_ref[...]
    gates = gates_ref[...]
    lane = _iota(gates.shape, 1)
    gcol = jnp.sum(jnp.where(lane == ROUTE_E0 + e, gates, 0.0), axis=1, keepdims=True)
    hmid = _silu(_dot(xb, wg_ref[0])) * _dot(xb, wu_ref[0]) * gcol
    acc_scr[...] += _dot(hmid.astype(BF16), wd_ref[0])

    @pl.when(e == N_EXPERTS - 1)
    def _():
        x2 = _layer_norm(DN_ALPHA * x1_ref[...] + acc_scr[...], ln_ref[0:1, :], ln_ref[1:2, :])
        x2_ref[...] = x2
        x2b_ref[...] = x2.astype(BF16)


def _const_spec(shape):
    nd = len(shape)
    return pl.BlockSpec(shape, lambda j: (0,) * nd)


def _row_spec(width):
    return pl.BlockSpec((ROW_BLK, width), lambda j: (j, 0))


def _state_in_spec(shape):
    nd = len(shape)
    return pl.BlockSpec((SEQ_BLK,) + shape[1:], lambda j: (jnp.maximum(j - N_CHUNK, 0),) + (0,) * (nd - 1))


def _state_out_spec(shape):
    nd = len(shape)
    return pl.BlockSpec((SEQ_BLK,) + shape[1:], lambda j: (jnp.maximum(j - N_CHUNK + 1, 0),) + (0,) * (nd - 1))


def _mixer_call(kernel_fn, name, x, xb, consts, states, width, extra_scratch=()):
    const_specs = [c[1] if isinstance(c, tuple) else _const_spec(c.shape) for c in consts]
    consts = [c[0] if isinstance(c, tuple) else c for c in consts]
    state_specs = [_state_in_spec(s.shape) for s in states]
    out_shapes = [jax.ShapeDtypeStruct((N_TOK, D_MODEL), F32), jax.ShapeDtypeStruct((N_TOK, D_MODEL), BF16),
                  jax.ShapeDtypeStruct((N_TOK, LANE), F32)]
    out_specs = [_row_spec(D_MODEL), _row_spec(D_MODEL), _row_spec(LANE)]
    for s in states:
        shp = (N_STATE_SEQ,) + s.shape[1:]
        out_shapes.append(jax.ShapeDtypeStruct(shp, F32))
        out_specs.append(_state_out_spec(shp))
    return pl.pallas_call(
        kernel_fn,
        grid=(N_STEPS,),
        in_specs=[_row_spec(D_MODEL), _row_spec(D_MODEL)] + const_specs + state_specs,
        out_specs=out_specs,
        out_shape=out_shapes,
        scratch_shapes=[pltpu.VMEM((ROW_BLK, width), F32), pltpu.VMEM((ROW_BLK, D_MODEL), F32)] + list(extra_scratch),
        compiler_params=pltpu.CompilerParams(dimension_semantics=("arbitrary",), vmem_limit_bytes=VMEM_LIMIT),
        name=name,
    )(x, xb, *consts, *states)


def _moe_call(x1, x1b, gates, wg, wu, wd, ln):
    return pl.pallas_call(
        _moe_kernel,
        grid=(N_STEPS, N_EXPERTS),
        in_specs=[pl.BlockSpec((ROW_BLK, D_MODEL), lambda i, e: (i, 0)),
                  pl.BlockSpec((ROW_BLK, D_MODEL), lambda i, e: (i, 0)),
                  pl.BlockSpec((ROW_BLK, LANE), lambda i, e: (i, 0)),
                  pl.BlockSpec((1, D_MODEL, D_EXPERT), lambda i, e: (e, 0, 0)),
                  pl.BlockSpec((1, D_MODEL, D_EXPERT), lambda i, e: (e, 0, 0)),
                  pl.BlockSpec((1, D_EXPERT, D_MODEL), lambda i, e: (e, 0, 0)),
                  pl.BlockSpec((2, D_MODEL), lambda i, e: (0, 0))],
        out_specs=[pl.BlockSpec((ROW_BLK, D_MODEL), lambda i, e: (i, 0)),
                   pl.BlockSpec((ROW_BLK, D_MODEL), lambda i, e: (i, 0))],
        out_shape=[jax.ShapeDtypeStruct((N_TOK, D_MODEL), F32), jax.ShapeDtypeStruct((N_TOK, D_MODEL), BF16)],
        scratch_shapes=[pltpu.VMEM((ROW_BLK, D_MODEL), F32)],
        compiler_params=pltpu.CompilerParams(dimension_semantics=("arbitrary", "arbitrary"),
                                             vmem_limit_bytes=VMEM_LIMIT),
        name="moe",
    )(x1, x1b, gates, wg, wu, wd, ln)


def _gla_sum_matrix():
    t = np.arange(CHUNK)
    mats = [(t[None, :] <= t[:, None])]
    for m in GLA_LEVELS:
        mid = (t // (2 * m)) * (2 * m) + m
        upper = (t % (2 * m)) >= m
        up = upper[:, None] & (t[None, :] > mid[:, None]) & (t[None, :] <= t[:, None])
        lo = (~upper)[:, None] & (t[None, :] > t[:, None]) & (t[None, :] <= mid[:, None])
        mats += [up, lo]
    return jnp.asarray(np.concatenate(mats, axis=0).astype(np.float32), dtype=BF16)


def _rope_table():
    inv = ROPE_BASE ** (-jnp.arange(0, DK, 2, dtype=F32) / DK)
    pos = jnp.concatenate([jnp.arange(SEQ, dtype=F32), PAST_LEN + jnp.arange(DEC_SEQ, dtype=F32)])
    ang = pos[:, None] * inv[None]
    cos, sin = jnp.cos(ang), jnp.sin(ang)
    cosf = jnp.tile(jnp.concatenate([cos, cos], axis=1), (1, N_HEADS))
    sinf = jnp.tile(jnp.concatenate([-sin, sin], axis=1), (1, N_HEADS))
    return jnp.concatenate([cosf, sinf], axis=1)


def _pad_lanes(a, width=LANE):
    return jnp.pad(a, [(0, 0)] * (a.ndim - 1) + [(0, width - a.shape[-1])])


def _router_pack(w_grp, b_grp, w_rt, b_rt):
    w = jnp.zeros((D_MODEL, LANE), F32).at[:, 0:N_GROUPS].set(w_grp).at[:, ROUTE_E0:ROUTE_E0 + N_EXPERTS].set(w_rt)
    wh = w.astype(BF16)
    wl = (w - wh.astype(F32)).astype(BF16)
    b = jnp.zeros((1, LANE), F32).at[0, 0:N_GROUPS].set(b_grp).at[0, ROUTE_E0:ROUTE_E0 + N_EXPERTS].set(b_rt)
    return jnp.concatenate([wh, wl], axis=1), b


def _to_blocks(x_prompt, x_sample):
    xp = x_prompt.reshape(BATCH, N_CHUNK, CHUNK, D_MODEL).transpose(1, 0, 2, 3).reshape(N_CHUNK * ROW_BLK, D_MODEL)
    return jnp.concatenate([xp, x_sample.reshape(DEC_BATCH * DEC_SEQ, D_MODEL)], axis=0)


def _from_blocks(x):
    yp = x[:N_CHUNK * ROW_BLK].reshape(N_CHUNK, BATCH, CHUNK, D_MODEL).transpose(1, 0, 2, 3).reshape(BATCH, SEQ, D_MODEL)
    return yp, x[N_CHUNK * ROW_BLK:].reshape(DEC_BATCH, DEC_SEQ, D_MODEL)


def kernel(x_prompt, x_sample, state_mlstm_c, state_mlstm_n, state_mlstm_m, state_gla, state_conv, state_rglru, state_ret, w_in_even, b_mlstm_i, b_mlstm_f, g_mlstm_norm, w_gla_a2, b_gla_a, g_gla_norm, w_out_even, w_in_odd, w_conv, b_conv, w_lru_r, b_lru_r, w_lru_i, b_lru_i, lru_lambda, g_ret_norm, w_out_odd, ln1_g, ln1_b, ln2_g, ln2_b, w_router_group, b_router_group, w_router_expert, b_router_expert, w_exp_gate, w_exp_up, w_exp_down):
    cm = _gla_sum_matrix()
    rope = _rope_table()
    rope = jnp.concatenate([rope[:SEQ]] + [rope[SEQ:]] * N_SAMPLE_BLK, axis=0)

    x = _to_blocks(x_prompt.astype(F32), x_sample.astype(F32))
    xb = x.astype(BF16)
    new = {k: [] for k in ("c", "n", "m", "g", "conv", "h", "r")}
    for layer in range(DEPTH):
        li = layer // 2
        wr, br = _router_pack(w_router_group[layer], b_router_group[layer], w_router_expert[layer],
                              b_router_expert[layer])
        ln1 = jnp.stack([ln1_g[layer], ln1_b[layer]])
        ln2 = jnp.stack([ln2_g[layer], ln2_b[layer]])
        if layer % 2 == 0:
            w = w_in_even[li]
            win = jnp.concatenate([w[:, 0:1024], w[:, 1032:1544], w[:, 1544:2568], w[:, 2584:3096], w[:, 1024:1032],
                                   w[:, 2568:2584], jnp.zeros((D_MODEL, EVEN_W - 3096), w.dtype)], axis=1).astype(BF16)
            bsm = jnp.zeros((1, LANE), F32).at[0, 0:N_HEADS].set(b_mlstm_i[li]).at[0, N_HEADS:2 * N_HEADS].set(b_mlstm_f[li])
            wa2 = jnp.zeros((LANE, N_HEADS * DK), F32).at[2 * N_HEADS:2 * N_HEADS + G_RANK].set(w_gla_a2[li]).astype(BF16)
            consts = [win, w_out_even[li].astype(BF16), ln1, wr, br, cm, bsm, g_mlstm_norm[li][None, :], wa2,
                      b_gla_a[li][None, :], g_gla_norm[li][None, :]]
            states = [state_mlstm_c[li].astype(F32), state_mlstm_n[li].astype(F32).reshape(DEC_BATCH, 1, N_HEADS * DK),
                      _pad_lanes(state_mlstm_m[li].astype(F32))[:, None, :], state_gla[li].astype(F32)]
            x1, x1b, gates, c, n, m, g = _mixer_call(_even_kernel, "even_mixer", x, xb, consts, states, EVEN_W)
            new["c"].append(c)
            new["n"].append(n.reshape(N_STATE_SEQ, N_HEADS, DK))
            new["m"].append(m[:, 0, 0:N_HEADS])
            new["g"].append(g)
        else:
            cw = jnp.concatenate([w_conv[li], b_conv[li][None, :]], axis=0)
            wr4 = w_lru_r[li].reshape(2, 4, R_BS, R_BS)
            wi4 = w_lru_i[li].reshape(2, 4, R_BS, R_BS)
            eye4 = jnp.eye(4, dtype=F32)
            bd = lambda w4: jnp.einsum("tncd,nm->tncmd", w4, eye4).reshape(2, 256, 256)
            wri = jnp.concatenate([bd(wr4), bd(wi4)], axis=2).astype(BF16)
            bri = jnp.stack([b_lru_r[li], b_lru_i[li]])
            consts = [w_in_odd[li].astype(BF16), w_out_odd[li].astype(BF16), ln1, wr, br,
                      (rope, pl.BlockSpec((CHUNK, 2 * N_HEADS * DK), lambda j: (j, 0))), cw, wri, bri,
                      lru_lambda[li][None, :], g_ret_norm[li][None, :]]
            states = [state_conv[li].astype(F32), state_rglru[li].astype(F32)[:, None, :], state_ret[li].astype(F32)]
            x1, x1b, gates, cv, hh, r = _mixer_call(_odd_kernel, "odd_mixer", x, xb, consts, states, ODD_W,
                                                    extra_scratch=[pltpu.VMEM((8 + CHUNK, R_WIDTH), F32)])
            new["conv"].append(cv)
            new["h"].append(hh[:, 0, :])
            new["r"].append(r)
        x, xb = _moe_call(x1, x1b, gates, w_exp_gate[layer].astype(BF16), w_exp_up[layer].astype(BF16),
                          w_exp_down[layer].astype(BF16), ln2)

    y_prompt, y_sample = _from_blocks(x)
    st = {k: jnp.stack(v) for k, v in new.items()}
    order = ("c", "n", "m", "g", "conv", "h", "r")
    dtypes = (state_mlstm_c.dtype, state_mlstm_n.dtype, state_mlstm_m.dtype, state_gla.dtype, state_conv.dtype,
              state_rglru.dtype, state_ret.dtype)
    prompt_states = tuple(st[k][:, :BATCH].astype(d) for k, d in zip(order, dtypes))
    sample_states = tuple(st[k][:, BATCH:].astype(d) for k, d in zip(order, dtypes))
    return (y_prompt.astype(x_prompt.dtype), y_sample.astype(x_sample.dtype)) + prompt_states + sample_states
```

```python
import functools
import math

import numpy as np
import jax
import jax.numpy as jnp
from jax import lax
from jax.experimental import pallas as pl
from jax.experimental.pallas import tpu as pltpu

F32 = jnp.float32
BF16 = jnp.bfloat16

D_MODEL = 1024
BATCH = 8
SEQ = 2048
DEPTH = 4
DEC_BATCH = 16
DEC_SEQ = 64
PAST_LEN = 1024
CHUNK = 64
HALF = D_MODEL // 2
N_HEADS = 4
DK = 64
DV = 128
G_RANK = 16
G_TAU = 16.0
R_WIDTH = HALF
R_BS = 64
CONV_W = 4
LRU_C = 8.0
ROPE_BASE = 10000.0
N_GROUPS = 4
EXPERTS_PER_GROUP = 4
N_EXPERTS = 16
D_EXPERT = D_MODEL // 4
DN_ALPHA = (2 * DEPTH) ** 0.25
LN_EPS = 1e-5
HEAD_EPS = 1e-6

SEQ_BLK = 8
ROW_BLK = SEQ_BLK * CHUNK
N_CHUNK = SEQ // CHUNK
N_SAMPLE_BLK = DEC_BATCH // SEQ_BLK
N_STEPS = N_CHUNK + N_SAMPLE_BLK
N_TOK = N_STEPS * ROW_BLK
N_STATE_SEQ = BATCH + DEC_BATCH

LANE = 128
EVEN_W = 3200
ODD_W = 2560
E_MQ, E_MK, E_MV, E_MO, E_GQ, E_GK, E_GV, E_GR, E_SM = 0, 256, 512, 1024, 1536, 1792, 2048, 2560, 3072
O_XB, O_GB, O_TQ, O_TK, O_TV, O_TG = 0, 512, 1024, 1280, 1536, 2048
GLA_LEVELS = (32, 16, 8, 4, 2, 1)
ROUTE_E0 = 16
PROJ_TILE = 640

PAIRS = ((0, 1), (0, 2), (0, 3), (1, 2), (1, 3), (2, 3))
N_CLASS = N_GROUPS * len(PAIRS)
WIN = 16
CAP_WIN = ROW_BLK // WIN + N_CLASS
CAP_ROWS = CAP_WIN * WIN
N_WIN = N_STEPS * CAP_WIN
WPT = 16
TILE_ROWS = WPT * WIN
N_TILES = -(-N_WIN // WPT) + N_CLASS
VMEM_LIMIT = 58 * 1024 * 1024


def _logsig(x):
    return jnp.minimum(x, 0.0) - jnp.log1p(jnp.exp(-jnp.abs(x)))


def _softplus(x):
    return jnp.maximum(x, 0.0) + jnp.log1p(jnp.exp(-jnp.abs(x)))


def _sigmoid(x):
    return 1.0 / (1.0 + jnp.exp(-x))


def _silu(x):
    return x * _sigmoid(x)


def _gelu_tanh(x):
    return 0.5 * x * (1.0 + jnp.tanh(math.sqrt(2.0 / math.pi) * (x + 0.044715 * (x * x * x))))


def _dot(a, b):
    return jnp.dot(a, b, preferred_element_type=F32)


def _dot_nt(a, b):
    return lax.dot_general(a, b, (((1,), (1,)), ((), ())), preferred_element_type=F32)


def _dot_tn(a, b):
    return lax.dot_general(a, b, (((0,), (0,)), ((), ())), preferred_element_type=F32)


def _split2(x):
    hi = x.astype(BF16)
    return hi, (x - hi.astype(F32)).astype(BF16)


def _split3(x):
    hi = x.astype(BF16)
    r1 = x - hi.astype(F32)
    mid = r1.astype(BF16)
    lo = (r1 - mid.astype(F32)).astype(BF16)
    return hi, mid, lo


def _rows_dot(mat01, x):
    hi, mid, lo = _split3(x)
    return _dot(mat01, hi) + _dot(mat01, mid) + _dot(mat01, lo)


def _head_norm(h, g):
    mu = jnp.mean(h, axis=-1, keepdims=True)
    d = h - mu
    var = jnp.mean(d * d, axis=-1, keepdims=True)
    return d * lax.rsqrt(var + HEAD_EPS) * g


def _layer_norm(x, g, b):
    mu = jnp.mean(x, axis=-1, keepdims=True)
    d = x - mu
    var = jnp.mean(d * d, axis=-1, keepdims=True)
    return d * lax.rsqrt(var + LN_EPS) * g + b


def _iota(shape, dim):
    return lax.broadcasted_iota(jnp.int32, shape, dim)


def _combine_ln(x1_ref, ys_ref, dsl_ref, ln_ref):
    dest = dsl_ref[:, 0:1]
    pos = _iota((ROW_BLK, CAP_ROWS), 1).astype(F32)
    sel = jnp.where(pos == dest, 1.0, 0.0).astype(BF16)
    moe = _dot(sel, ys_ref[...])
    return _layer_norm(DN_ALPHA * x1_ref[...] + moe, ln_ref[0:1, :], ln_ref[1:2, :])


def _load_x(first, head_refs, x_scr):
    j = pl.program_id(0)
    if first:
        xp_ref, xsm_ref = head_refs

        @pl.when(j < N_CHUNK)
        def _():
            x_scr[...] = xp_ref[...].reshape(ROW_BLK, D_MODEL)

        @pl.when(j >= N_CHUNK)
        def _():
            x_scr[...] = xsm_ref[...].reshape(ROW_BLK, D_MODEL)
    else:
        x_scr[...] = _combine_ln(*head_refs)


def _project_in(x_scr, win_ref, z_scr, width):
    xb = x_scr[...].astype(BF16)
    for c0 in range(0, width, PROJ_TILE):
        z_scr[:, c0:c0 + PROJ_TILE] = _dot(xb, win_ref[:, c0:c0 + PROJ_TILE])


def _route(x1, xh, wr_ref, br_ref):
    xl = (x1 - xh.astype(F32)).astype(BF16)
    wh = wr_ref[:, 0:LANE]
    wl = wr_ref[:, LANE:2 * LANE]
    logits = _dot(xh, wh) + _dot(xh, wl) + _dot(xl, wh) + br_ref[...]
    lane = _iota(logits.shape, 1).astype(F32)
    neg = -jnp.inf
    big = float(4 * LANE)
    gl = jnp.where(lane < N_GROUPS, logits, neg)
    gmax = jnp.max(gl, axis=1, keepdims=True)
    gsum = jnp.sum(jnp.exp(gl - gmax), axis=1, keepdims=True)
    g_p = 1.0 / gsum
    g_idx = jnp.min(jnp.where(gl == gmax, lane, big), axis=1, keepdims=True)
    lo = ROUTE_E0 + EXPERTS_PER_GROUP * g_idx
    in_grp = (lane >= lo) & (lane < lo + EXPERTS_PER_GROUP)
    el = jnp.where(in_grp, logits, neg)
    e1 = jnp.max(el, axis=1, keepdims=True)
    i1 = jnp.min(jnp.where(el == e1, lane, big), axis=1, keepdims=True)
    el2 = jnp.where(lane == i1, neg, el)
    e2 = jnp.max(el2, axis=1, keepdims=True)
    i2 = jnp.min(jnp.where(el2 == e2, lane, big), axis=1, keepdims=True)
    t = jnp.exp(e2 - e1)
    w1 = g_p / (1.0 + t)
    w2 = g_p * t / (1.0 + t)
    first_lower = i1 < i2
    la = jnp.minimum(i1, i2) - lo
    lb = jnp.maximum(i1, i2) - lo
    pair = la * (7.0 - la) * 0.5 + (lb - la - 1.0)
    cls = g_idx * float(len(PAIRS)) + pair
    return cls, jnp.where(first_lower, w1, w2), jnp.where(first_lower, w2, w1)


def _mixer_tail(x_scr, y_scr, wout_ref, ln_ref, wr_ref, br_ref, lt_ref, ut_ref,
                x1_ref, xs_ref, gs_ref, dsl_ref, meta_ref):
    y = _dot(y_scr[...].astype(BF16), wout_ref[...])
    x1 = _layer_norm(DN_ALPHA * x_scr[...] + y, ln_ref[0:1, :], ln_ref[1:2, :])
    x1_ref[...] = x1
    x1b = x1.astype(BF16)
    cls, g_a, g_b = _route(x1, x1b, wr_ref, br_ref)

    lane = _iota((ROW_BLK, LANE), 1).astype(F32)
    onehot = jnp.where(lane == cls, 1.0, 0.0)
    earlier = _dot(lt_ref[...], onehot.astype(BF16))
    rank = jnp.sum(earlier * onehot, axis=1, keepdims=True)
    cnt = jnp.sum(onehot, axis=0, keepdims=True)
    nwin = jnp.floor((cnt + (WIN - 1.0)) * (1.0 / WIN))
    woff = _dot(jnp.broadcast_to(nwin, (8, LANE)).astype(BF16), ut_ref[...])[0:1]
    dest = WIN * jnp.sum(onehot * woff, axis=1, keepdims=True) + rank
    pos = _iota((ROW_BLK, CAP_ROWS), 1).astype(F32)
    sel = jnp.where(pos == dest, 1.0, 0.0).astype(BF16)
    for c0 in range(0, D_MODEL, 256):
        xs_ref[:, c0:c0 + 256] = _dot_tn(sel, x1b[:, c0:c0 + 256]).astype(BF16)
    ga_hi, ga_lo = _split2(g_a)
    gb_hi, gb_lo = _split2(g_b)
    gpack = jnp.where(lane == 0, ga_hi.astype(F32), jnp.where(lane == 1, ga_lo.astype(F32),
            jnp.where(lane == 2, gb_hi.astype(F32), jnp.where(lane == 3, gb_lo.astype(F32), 0.0))))
    gs_ref[...] = _dot_tn(sel, gpack.astype(BF16))
    dsl_ref[...] = jnp.where(lane == 0, dest, 0.0)
    meta_ref[...] = jnp.concatenate([cnt, nwin, woff, jnp.zeros((5, LANE), F32)], axis=0)


def _init_states(j, pairs):
    @pl.when(j == 0)
    def _():
        for _, out_ref in pairs:
            out_ref[...] = jnp.zeros(out_ref.shape, out_ref.dtype)

    @pl.when(j >= N_CHUNK)
    def _():
        for in_ref, out_ref in pairs:
            out_ref[...] = in_ref[...]


def _even_kernel(*refs, first):
    nh = 2 if first else 4
    head_refs = refs[:nh]
    (win_ref, wout_ref, ln_ref, wr_ref, br_ref, lt_ref, ut_ref, cm_ref, bsm_ref, gm_ref, wa2_ref, ba_ref, gg_ref,
     c_in, n_in, m_in, s_in,
     x1_ref, xs_ref, gs_ref, dsl_ref, meta_ref, c_out, n_out, m_out, s_out,
     x_scr, z_scr, y_scr) = refs[nh:]
    j = pl.program_id(0)
    _init_states(j, ((c_in, c_out), (n_in, n_out), (m_in, m_out), (s_in, s_out)))
    _load_x(first, head_refs, x_scr)
    _project_in(x_scr, win_ref, z_scr, EVEN_W)

    rr = _iota((CHUNK, CHUNK), 0)
    cc = _iota((CHUNK, CHUNK), 1)
    causal = cc <= rr
    eye = cc == rr
    trow = _iota((CHUNK, 1), 0)
    tri = cm_ref[0:CHUNK, :]
    neg = -jnp.inf

    def seq_body(s, carry):
        rows = pl.ds(pl.multiple_of(s * CHUNK, CHUNK), CHUNK)
        n_all = n_out[s]
        m_all = m_out[s]
        lane_m = _iota((1, LANE), 1)
        n_new = []
        m_new = m_all
        slab = z_scr[rows, E_SM:E_SM + LANE]
        pre = slab + bsm_ref[...]
        fcum = _rows_dot(tri, _logsig(pre))
        at = jnp.concatenate([pre, fcum], axis=0).T

        for h in range(N_HEADS):
            bc = fcum[:, N_HEADS + h:N_HEADS + h + 1]
            icol = pre[:, h:h + 1]
            br = at[N_HEADS + h:N_HEADS + h + 1, CHUNK:2 * CHUNK]
            ir = at[h:h + 1, 0:CHUNK]
            mprev = m_all[:, h:h + 1]
            logd = jnp.where(causal, bc - br + ir, neg)
            linter = bc + mprev
            mrow = jnp.maximum(linter, jnp.max(logd, axis=1, keepdims=True))
            wintra = jnp.exp(logd - mrow)
            winter = jnp.exp(linter - mrow)
            qh = z_scr[rows, E_MQ + h * DK:E_MQ + (h + 1) * DK]
            kh = z_scr[rows, E_MK + h * DK:E_MK + (h + 1) * DK] * (DK ** -0.5)
            vb = z_scr[rows, E_MV + h * DV:E_MV + (h + 1) * DV].astype(BF16)
            qb = qh.astype(BF16)
            sm = _dot_nt(qb, kh.astype(BF16)) * wintra
            cst = c_out[s, h]
            nrow = n_all[:, h * DK:(h + 1) * DK]
            num = _dot(sm.astype(BF16), vb) + winter * _dot(qb, cst.astype(BF16))
            den = jnp.sum(sm, axis=1, keepdims=True) + winter * jnp.sum(qh * nrow, axis=1, keepdims=True)
            hh = num / jnp.maximum(jnp.abs(den), jnp.exp(-mrow))
            blast = bc[CHUNK - 1:CHUNK, :]
            logw = blast - bc + icol
            mnew = jnp.maximum(blast + mprev, jnp.max(logw, axis=0, keepdims=True))
            kw = kh * jnp.exp(logw - mnew)
            decay = jnp.exp(blast + mprev - mnew)
            c_out[s, h] = decay * cst + _dot_tn(kw.astype(BF16), vb)
            n_new.append(decay * nrow + jnp.sum(kw, axis=0, keepdims=True))
            m_new = jnp.where(lane_m == h, mnew, m_new)
            mo = z_scr[rows, E_MO + h * DV:E_MO + (h + 1) * DV]
            y_scr[rows, h * DV:(h + 1) * DV] = _sigmoid(mo) * _head_norm(hh, gm_ref[:, h * DV:(h + 1) * DV])

        n_out[s] = jnp.concatenate(n_new, axis=1)
        m_out[s] = m_new

        la = _logsig(_dot(slab.astype(BF16), wa2_ref[...]) + ba_ref[...]) * (1.0 / G_TAU)
        ex = _rows_dot(cm_ref[...], la)
        bg = ex[0:CHUNK]
        gq = z_scr[rows, E_GQ:E_GQ + N_HEADS * DK] * (DK ** -0.5)
        gk = z_scr[rows, E_GK:E_GK + N_HEADS * DK]
        blast = bg[CHUNK - 1:CHUNK, :]
        qdec = (gq * jnp.exp(bg)).astype(BF16)
        kdec = (gk * jnp.exp(blast - bg)).astype(BF16)
        lat = jnp.concatenate([la, jnp.zeros_like(la)], axis=0).T
        sdec = jnp.exp(jnp.sum(lat, axis=1, keepdims=True))
        gqb = gq.astype(BF16)
        gkb = gk.astype(BF16)
        qts, kts = [], []
        for li, m in enumerate(GLA_LEVELS):
            r0 = CHUNK * (1 + 2 * li)
            up = (trow & (2 * m - 1)) >= m
            qts.append(jnp.where(up, gq * jnp.exp(ex[r0:r0 + CHUNK]), 0.0).astype(BF16))
            kts.append(jnp.where(up, 0.0, gk * jnp.exp(ex[r0 + CHUNK:r0 + 2 * CHUNK])).astype(BF16))
        for h in range(N_HEADS):
            hs = slice(h * DK, (h + 1) * DK)
            att = jnp.where(eye, _dot_nt(gqb[:, hs], gkb[:, hs]), 0.0)
            for li, m in enumerate(GLA_LEVELS):
                sh = int(math.log2(2 * m))
                same = (rr >> sh) == (cc >> sh)
                att = att + jnp.where(same, _dot_nt(qts[li][:, hs], kts[li][:, hs]), 0.0)
            vb = z_scr[rows, E_GV + h * DV:E_GV + (h + 1) * DV].astype(BF16)
            sst = s_out[s, h]
            og = _dot(att.astype(BF16), vb) + _dot(qdec[:, hs], sst.astype(BF16))
            s_out[s, h] = sdec[h * DK:(h + 1) * DK, :] * sst + _dot_tn(kdec[:, hs], vb)
            gr = z_scr[rows, E_GR + h * DV:E_GR + (h + 1) * DV]
            y_scr[rows, HALF + h * DV:HALF + (h + 1) * DV] = _silu(gr) * _head_norm(og, gg_ref[:, h * DV:(h + 1) * DV])
        return carry

    lax.fori_loop(0, SEQ_BLK, seq_body, 0)
    _mixer_tail(x_scr, y_scr, wout_ref, ln_ref, wr_ref, br_ref, lt_ref, ut_ref,
                x1_ref, xs_ref, gs_ref, dsl_ref, meta_ref)


def _odd_kernel(*refs, first):
    nh = 2 if first else 4
    head_refs = refs[:nh]
    (win_ref, wout_ref, ln_ref, wr_ref, br_ref, lt_ref, ut_ref, rope_ref, cw_ref, wri_ref, bri_ref, lam_ref, gt_ref,
     cv_in, h_in, r_in,
     x1_ref, xs_ref, gs_ref, dsl_ref, meta_ref, cv_out, h_out, r_out,
     x_scr, z_scr, y_scr, pad_scr) = refs[nh:]
    j = pl.program_id(0)
    _init_states(j, ((cv_in, cv_out), (h_in, h_out), (r_in, r_out)))
    _load_x(first, head_refs, x_scr)
    _project_in(x_scr, win_ref, z_scr, ODD_W)

    rr = _iota((CHUNK, CHUNK), 0)
    cc = _iota((CHUNK, CHUNK), 1)
    causal = cc <= rr
    dist = (rr - cc).astype(F32)
    trow = _iota((CHUNK, 1), 0)
    tf = trow.astype(F32)
    lane4 = _iota((CHUNK, N_HEADS * DK), 1)
    first_half = (lane4 & (DK - 1)) < (DK // 2)
    lgs = [math.log1p(-(2.0 ** (-5.0 - h))) for h in range(N_HEADS)]
    sp = _softplus(-lam_ref[...])
    cosf = rope_ref[:, 0:N_HEADS * DK]
    sinf = rope_ref[:, N_HEADS * DK:2 * N_HEADS * DK]

    def rot(a):
        swapped = jnp.where(first_half, pltpu.roll(a, N_HEADS * DK - DK // 2, 1), pltpu.roll(a, DK // 2, 1))
        return a * cosf + swapped * sinf

    def seq_body(s, carry):
        rows = pl.ds(pl.multiple_of(s * CHUNK, CHUNK), CHUNK)
        xb = z_scr[rows, O_XB:O_XB + R_WIDTH]
        pad_scr[8 - (CONV_W - 1):8, :] = cv_out[s]
        pad_scr[8:8 + CHUNK, :] = xb
        xc = cw_ref[CONV_W:CONV_W + 1, :] + xb * cw_ref[CONV_W - 1:CONV_W, :]
        for d in range(1, CONV_W):
            xc = xc + pad_scr[8 - d:8 - d + CHUNK, :] * cw_ref[CONV_W - 1 - d:CONV_W - d, :]
        cv_out[s] = xb[CHUNK - (CONV_W - 1):CHUNK, :]
        xcb = xc.astype(BF16)
        halves = []
        for t in range(2):
            halves.append(_dot(xcb[:, t * 256:(t + 1) * 256], wri_ref[t]))
        r_pre = jnp.concatenate([halves[0][:, 0:256], halves[1][:, 0:256]], axis=1) + bri_ref[0:1, :]
        i_pre = jnp.concatenate([halves[0][:, 256:512], halves[1][:, 256:512]], axis=1) + bri_ref[1:2, :]
        log_a = (-LRU_C) * _sigmoid(r_pre) * sp
        a = jnp.exp(log_a)
        th = jnp.tanh(log_a)
        u = jnp.sqrt((-2.0 * th) / (1.0 - th)) * (_sigmoid(i_pre) * xc)
        u = u + jnp.where(trow == 0, a * h_out[s], 0.0)
        for sh in (1, 2, 4, 8, 16, 32):
            valid = trow >= sh
            u_prev = pltpu.roll(u, sh, 0)
            a_prev = pltpu.roll(a, sh, 0)
            u = jnp.where(valid, a * u_prev + u, u)
            a = jnp.where(valid, a * a_prev, a)
        h_out[s] = u[CHUNK - 1:CHUNK, :]
        y_scr[rows, 0:R_WIDTH] = _gelu_tanh(z_scr[rows, O_GB:O_GB + R_WIDTH]) * u

        q = rot(z_scr[rows, O_TQ:O_TQ + N_HEADS * DK]).astype(BF16)
        kf = rot(z_scr[rows, O_TK:O_TK + N_HEADS * DK]) * (DK ** -0.5)
        for h in range(N_HEADS):
            hs = slice(h * DK, (h + 1) * DK)
            lg = lgs[h]
            dec = jnp.where(causal, jnp.exp(dist * lg), 0.0)
            w_inter = jnp.exp((tf + 1.0) * lg)
            w_key = jnp.exp((CHUNK - 1.0 - tf) * lg)
            s_decay = math.exp(CHUNK * lg)
            qh = q[:, hs]
            kh = kf[:, hs]
            vb = z_scr[rows, O_TV + h * DV:O_TV + (h + 1) * DV].astype(BF16)
            sst = r_out[s, h]
            att = _dot_nt(qh, kh.astype(BF16)) * dec
            o = _dot(att.astype(BF16), vb) + w_inter * _dot(qh, sst.astype(BF16))
            r_out[s, h] = s_decay * sst + _dot_tn((kh * w_key).astype(BF16), vb)
            tg = z_scr[rows, O_TG + h * DV:O_TG + (h + 1) * DV]
            y_scr[rows, HALF + h * DV:HALF + (h + 1) * DV] = _silu(tg) * _head_norm(o, gt_ref[:, h * DV:(h + 1) * DV])
        return carry

    lax.fori_loop(0, SEQ_BLK, seq_body, 0)
    _mixer_tail(x_scr, y_scr, wout_ref, ln_ref, wr_ref, br_ref, lt_ref, ut_ref,
                x1_ref, xs_ref, gs_ref, dsl_ref, meta_ref)


def _moe_kernel(nused_ref, ea_ref, eb_ref, wid_ref, xs_hbm, gs_hbm, wga_ref, wua_ref, wda_ref, wgb_ref, wub_ref,
                wdb_ref, ys_hbm, xbuf, gbuf, ybuf, sem_x, sem_g, sem_y):
    i = pl.program_id(0)
    n_used = nused_ref[0]
    slot = i % 2

    def x_copy(w, sl, k):
        return pltpu.make_async_copy(xs_hbm.at[w], xbuf.at[sl, k], sem_x.at[sl, k])

    def g_copy(w, sl, k):
        return pltpu.make_async_copy(gs_hbm.at[w], gbuf.at[sl, k], sem_g.at[sl, k])

    def y_copy(w, sl, k):
        return pltpu.make_async_copy(ybuf.at[sl, k], ys_hbm.at[w], sem_y.at[sl, k])

    def for_windows(tile, fn):
        for k in range(WPT):
            w = wid_ref[tile * WPT + k]

            @pl.when(w >= 0)
            def _():
                fn(w, k)

    def start_gather(tile, sl):
        def fn(w, k):
            x_copy(w, sl, k).start()
            g_copy(w, sl, k).start()
        for_windows(tile, fn)

    @pl.when(i == 0)
    def _():
        xbuf[...] = jnp.zeros(xbuf.shape, xbuf.dtype)
        gbuf[...] = jnp.zeros(gbuf.shape, gbuf.dtype)
        start_gather(0, 0)

    @pl.when(i + 1 < n_used)
    def _():
        start_gather(i + 1, 1 - slot)

    @pl.when(i < n_used)
    def _():
        def wait_in(w, k):
            x_copy(w, slot, k).wait()
            g_copy(w, slot, k).wait()
        for_windows(i, wait_in)

        @pl.when(i >= 2)
        def _():
            for_windows(i - 2, lambda w, k: y_copy(w, slot, k).wait())

        x = xbuf[slot].reshape(TILE_ROWS, D_MODEL)
        g = gbuf[slot].reshape(TILE_ROWS, LANE)
        g_a = g[:, 0:1] + g[:, 1:2]
        g_b = g[:, 2:3] + g[:, 3:4]
        h_a = _silu(_dot(x, wga_ref[0])) * _dot(x, wua_ref[0]) * g_a
        h_b = _silu(_dot(x, wgb_ref[0])) * _dot(x, wub_ref[0]) * g_b
        y = _dot(h_a.astype(BF16), wda_ref[0]) + _dot(h_b.astype(BF16), wdb_ref[0])
        ybuf[slot] = y.astype(BF16).reshape(WPT, WIN, D_MODEL)
        for_windows(i, lambda w, k: y_copy(w, slot, k).start())

        @pl.when(i == n_used - 1)
        def _():
            for_windows(i, lambda w, k: y_copy(w, slot, k).wait())

            @pl.when(i >= 1)
            def _():
                for_windows(i - 1, lambda w, k: y_copy(w, 1 - slot, k).wait())


def _final_kernel(x1_ref, ys_ref, dsl_ref, ln_ref, yp_ref, ysm_ref):
    j = pl.program_id(0)
    x2 = _combine_ln(x1_ref, ys_ref, dsl_ref, ln_ref).reshape(SEQ_BLK, CHUNK, D_MODEL)

    @pl.when(j < N_CHUNK)
    def _():
        yp_ref[...] = x2

    @pl.when(j >= N_CHUNK)
    def _():
        ysm_ref[...] = x2


def _const_spec(shape):
    nd = len(shape)
    return pl.BlockSpec(shape, lambda j: (0,) * nd, pipeline_mode=pl.Buffered(1))


def _row_spec(rows, width):
    return pl.BlockSpec((rows, width), lambda j: (j, 0))


def _prompt_spec():
    return pl.BlockSpec((SEQ_BLK, CHUNK, D_MODEL), lambda j: (0, jnp.minimum(j, N_CHUNK - 1), 0))


def _sample_spec():
    return pl.BlockSpec((SEQ_BLK, CHUNK, D_MODEL), lambda j: (jnp.maximum(j - N_CHUNK, 0), 0, 0))


def _state_in_spec(shape):
    nd = len(shape)
    return pl.BlockSpec((SEQ_BLK,) + shape[1:], lambda j: (jnp.maximum(j - N_CHUNK, 0),) + (0,) * (nd - 1))


def _state_out_spec(shape):
    nd = len(shape)
    return pl.BlockSpec((SEQ_BLK,) + shape[1:], lambda j: (jnp.maximum(j - N_CHUNK + 1, 0),) + (0,) * (nd - 1))


def _head_specs(first):
    if first:
        return [_prompt_spec(), _sample_spec()]
    return [_row_spec(ROW_BLK, D_MODEL), _row_spec(CAP_ROWS, D_MODEL), _row_spec(ROW_BLK, LANE),
            _const_spec((2, D_MODEL))]


def _mixer_call(kernel_fn, name, first, head, consts, states, width, extra_scratch=()):
    const_specs = [c[1] if isinstance(c, tuple) else _const_spec(c.shape) for c in consts]
    consts = [c[0] if isinstance(c, tuple) else c for c in consts]
    state_specs = [_state_in_spec(s.shape) for s in states]
    out_shapes = [jax.ShapeDtypeStruct((N_TOK, D_MODEL), F32),
                  jax.ShapeDtypeStruct((N_STEPS * CAP_ROWS, D_MODEL), BF16),
                  jax.ShapeDtypeStruct((N_STEPS * CAP_ROWS, LANE), F32),
                  jax.ShapeDtypeStruct((N_TOK, LANE), F32),
                  jax.ShapeDtypeStruct((N_STEPS * 8, LANE), F32)]
    out_specs = [_row_spec(ROW_BLK, D_MODEL), _row_spec(CAP_ROWS, D_MODEL), _row_spec(CAP_ROWS, LANE),
                 _row_spec(ROW_BLK, LANE), _row_spec(8, LANE)]
    for s in states:
        shp = (N_STATE_SEQ,) + s.shape[1:]
        out_shapes.append(jax.ShapeDtypeStruct(shp, F32))
        out_specs.append(_state_out_spec(shp))
    return pl.pallas_call(
        functools.partial(kernel_fn, first=first),
        grid=(N_STEPS,),
        in_specs=_head_specs(first) + const_specs + state_specs,
        out_specs=out_specs,
        out_shape=out_shapes,
        scratch_shapes=[pltpu.VMEM((ROW_BLK, D_MODEL), F32), pltpu.VMEM((ROW_BLK, width), F32),
                        pltpu.VMEM((ROW_BLK, D_MODEL), F32)] + list(extra_scratch),
        compiler_params=pltpu.CompilerParams(dimension_semantics=("arbitrary",), vmem_limit_bytes=VMEM_LIMIT),
        name=name,
    )(*head, *consts, *states)


def _moe_call(n_used, tile_ea, tile_eb, wid, xs, gs, wg, wu, wd):
    xs_w = xs.reshape(N_WIN, WIN, D_MODEL)
    gs_w = gs.reshape(N_WIN, WIN, LANE)
    wspec = lambda shape, sel: pl.BlockSpec((1,) + shape, lambda i, nu, ea, eb, wi: ((ea, eb)[sel][i], 0, 0))
    grid_spec = pltpu.PrefetchScalarGridSpec(
        num_scalar_prefetch=4,
        grid=(N_TILES,),
        in_specs=[pl.BlockSpec(memory_space=pl.ANY), pl.BlockSpec(memory_space=pl.ANY),
                  wspec((D_MODEL, D_EXPERT), 0), wspec((D_MODEL, D_EXPERT), 0), wspec((D_EXPERT, D_MODEL), 0),
                  wspec((D_MODEL, D_EXPERT), 1), wspec((D_MODEL, D_EXPERT), 1), wspec((D_EXPERT, D_MODEL), 1)],
        out_specs=pl.BlockSpec(memory_space=pl.ANY),
        scratch_shapes=[pltpu.VMEM((2, WPT, WIN, D_MODEL), BF16), pltpu.VMEM((2, WPT, WIN, LANE), F32),
                        pltpu.VMEM((2, WPT, WIN, D_MODEL), BF16),
                        pltpu.SemaphoreType.DMA((2, WPT)), pltpu.SemaphoreType.DMA((2, WPT)),
                        pltpu.SemaphoreType.DMA((2, WPT))])
    ys_w = pl.pallas_call(
        _moe_kernel,
        grid_spec=grid_spec,
        out_shape=jax.ShapeDtypeStruct((N_WIN, WIN, D_MODEL), BF16),
        input_output_aliases={4: 0},
        compiler_params=pltpu.CompilerParams(dimension_semantics=("arbitrary",), vmem_limit_bytes=VMEM_LIMIT),
        name="moe",
    )(n_used, tile_ea, tile_eb, wid, xs_w, gs_w, wg, wu, wd, wg, wu, wd)
    return ys_w.reshape(N_STEPS * CAP_ROWS, D_MODEL)


def _final_call(x1, ys, dsl, ln):
    return pl.pallas_call(
        _final_kernel,
        grid=(N_STEPS,),
        in_specs=[_row_spec(ROW_BLK, D_MODEL), _row_spec(CAP_ROWS, D_MODEL), _row_spec(ROW_BLK, LANE),
                  _const_spec((2, D_MODEL))],
        out_specs=[_prompt_spec(), _sample_spec()],
        out_shape=[jax.ShapeDtypeStruct((BATCH, SEQ, D_MODEL), F32),
                   jax.ShapeDtypeStruct((DEC_BATCH, DEC_SEQ, D_MODEL), F32)],
        compiler_params=pltpu.CompilerParams(dimension_semantics=("arbitrary",), vmem_limit_bytes=VMEM_LIMIT),
        name="final_ln",
    )(x1, ys, dsl, ln)


def _moe_tables(meta):
    i32 = jnp.int32
    meta = meta.reshape(N_STEPS, 8, LANE)
    nwin = meta[:, 1, :N_CLASS].astype(i32)
    woff = meta[:, 2, :N_CLASS].astype(i32)
    cum_incl = jnp.cumsum(nwin, axis=0)
    cum_excl = cum_incl - nwin
    per_class = cum_incl[-1]
    tiles_per_class = (per_class + WPT - 1) // WPT
    tile_end = jnp.cumsum(tiles_per_class)
    tile_start = tile_end - tiles_per_class
    n_used = tile_end[-1]
    tile = jnp.arange(N_TILES, dtype=i32)
    tile_cls = jnp.minimum(jnp.sum((tile[:, None] >= tile_end[None, :]).astype(i32), axis=1), N_CLASS - 1)
    last_cls = tile_cls[jnp.maximum(n_used - 1, 0)]
    tile_cls = jnp.where(tile < n_used, tile_cls, last_cls)
    ordinal = (tile - tile_start[tile_cls])[:, None] * WPT + jnp.arange(WPT, dtype=i32)[None, :]
    valid = (tile[:, None] < n_used) & (ordinal < per_class[tile_cls][:, None])
    cum_c = cum_incl.T[tile_cls]
    blk = jnp.sum((ordinal[:, :, None] >= cum_c[:, None, :]).astype(i32), axis=2)
    blk = jnp.minimum(blk, N_STEPS - 1)
    cls2 = jnp.broadcast_to(tile_cls[:, None], blk.shape)
    wid = blk * CAP_WIN + woff[blk, cls2] + ordinal - cum_excl[blk, cls2]
    wid = jnp.where(valid, wid, -1).astype(i32).reshape(-1)
    pair_a = jnp.asarray([p[0] for p in PAIRS], i32)
    pair_b = jnp.asarray([p[1] for p in PAIRS], i32)
    grp = tile_cls // len(PAIRS)
    tile_ea = grp * EXPERTS_PER_GROUP + pair_a[tile_cls % len(PAIRS)]
    tile_eb = grp * EXPERTS_PER_GROUP + pair_b[tile_cls % len(PAIRS)]
    return n_used.reshape(1).astype(i32), tile_ea.astype(i32), tile_eb.astype(i32), wid


def _gla_sum_matrix():
    t = np.arange(CHUNK)
    mats = [(t[None, :] <= t[:, None])]
    for m in GLA_LEVELS:
        mid = (t // (2 * m)) * (2 * m) + m
        upper = (t % (2 * m)) >= m
        up = upper[:, None] & (t[None, :] > mid[:, None]) & (t[None, :] <= t[:, None])
        lo = (~upper)[:, None] & (t[None, :] > t[:, None]) & (t[None, :] <= mid[:, None])
        mats += [up, lo]
    return jnp.asarray(np.concatenate(mats, axis=0).astype(np.float32), dtype=BF16)


def _rope_table():
    inv = ROPE_BASE ** (-jnp.arange(0, DK, 2, dtype=F32) / DK)
    pos = jnp.concatenate([jnp.arange(SEQ, dtype=F32)] + [PAST_LEN + jnp.arange(DEC_SEQ, dtype=F32)] * N_SAMPLE_BLK)
    ang = pos[:, None] * inv[None]
    cos, sin = jnp.cos(ang), jnp.sin(ang)
    cosf = jnp.tile(jnp.concatenate([cos, cos], axis=1), (1, N_HEADS))
    sinf = jnp.tile(jnp.concatenate([-sin, sin], axis=1), (1, N_HEADS))
    return jnp.concatenate([cosf, sinf], axis=1)


def _pad_lanes(a, width=LANE):
    return jnp.pad(a, [(0, 0)] * (a.ndim - 1) + [(0, width - a.shape[-1])])


def _router_pack(w_grp, b_grp, w_rt, b_rt):
    w = jnp.zeros((D_MODEL, LANE), F32).at[:, 0:N_GROUPS].set(w_grp).at[:, ROUTE_E0:ROUTE_E0 + N_EXPERTS].set(w_rt)
    wh = w.astype(BF16)
    wl = (w - wh.astype(F32)).astype(BF16)
    b = jnp.zeros((1, LANE), F32).at[0, 0:N_GROUPS].set(b_grp).at[0, ROUTE_E0:ROUTE_E0 + N_EXPERTS].set(b_rt)
    return jnp.concatenate([wh, wl], axis=1), b


def kernel(x_prompt, x_sample, state_mlstm_c, state_mlstm_n, state_mlstm_m, state_gla, state_conv, state_rglru, state_ret, w_in_even, b_mlstm_i, b_mlstm_f, g_mlstm_norm, w_gla_a2, b_gla_a, g_gla_norm, w_out_even, w_in_odd, w_conv, b_conv, w_lru_r, b_lru_r, w_lru_i, b_lru_i, lru_lambda, g_ret_norm, w_out_odd, ln1_g, ln1_b, ln2_g, ln2_b, w_router_group, b_router_group, w_router_expert, b_router_expert, w_exp_gate, w_exp_up, w_exp_down):
    cm = _gla_sum_matrix()
    rope = _rope_table()
    tt = np.arange(ROW_BLK)
    lower_strict = jnp.asarray((tt[None, :] < tt[:, None]).astype(np.float32), dtype=BF16)
    ll = np.arange(LANE)
    upper_strict = jnp.asarray((ll[:, None] < ll[None, :]).astype(np.float32), dtype=BF16)

    head = [x_prompt.astype(F32), x_sample.astype(F32)]
    new = {k: [] for k in ("c", "n", "m", "g", "conv", "h", "r")}
    for layer in range(DEPTH):
        li = layer // 2
        first = layer == 0
        wr, br = _router_pack(w_router_group[layer], b_router_group[layer], w_router_expert[layer],
                              b_router_expert[layer])
        ln1 = jnp.stack([ln1_g[layer], ln1_b[layer]])
        common = [ln1, wr, br, lower_strict, upper_strict]
        if layer % 2 == 0:
            w = w_in_even[li]
            win = jnp.concatenate([w[:, 0:1024], w[:, 1032:1544], w[:, 1544:2568], w[:, 2584:3096], w[:, 1024:1032],
                                   w[:, 2568:2584], jnp.zeros((D_MODEL, EVEN_W - 3096), w.dtype)], axis=1).astype(BF16)
            bsm = jnp.zeros((1, LANE), F32).at[0, 0:N_HEADS].set(b_mlstm_i[li]).at[0, N_HEADS:2 * N_HEADS].set(b_mlstm_f[li])
            wa2 = jnp.zeros((LANE, N_HEADS * DK), F32).at[2 * N_HEADS:2 * N_HEADS + G_RANK].set(w_gla_a2[li]).astype(BF16)
            consts = [win, w_out_even[li].astype(BF16)] + common + [cm, bsm, g_mlstm_norm[li][None, :], wa2,
                                                                     b_gla_a[li][None, :], g_gla_norm[li][None, :]]
            states = [state_mlstm_c[li].astype(F32), state_mlstm_n[li].astype(F32).reshape(DEC_BATCH, 1, N_HEADS * DK),
                      _pad_lanes(state_mlstm_m[li].astype(F32))[:, None, :], state_gla[li].astype(F32)]
            x1, xs, gs, dsl, meta, c, n, m, g = _mixer_call(_even_kernel, "even_mixer", first, head, consts, states,
                                                            EVEN_W)
            new["c"].append(c)
            new["n"].append(n.reshape(N_STATE_SEQ, N_HEADS, DK))
            new["m"].append(m[:, 0, 0:N_HEADS])
            new["g"].append(g)
        else:
            cw = jnp.concatenate([w_conv[li], b_conv[li][None, :]], axis=0)
            wr4 = w_lru_r[li].reshape(2, 4, R_BS, R_BS)
            wi4 = w_lru_i[li].reshape(2, 4, R_BS, R_BS)
            eye4 = jnp.eye(4, dtype=F32)
            bd = lambda w4: jnp.einsum("tncd,nm->tncmd", w4, eye4).reshape(2, 256, 256)
            wri = jnp.concatenate([bd(wr4), bd(wi4)], axis=2).astype(BF16)
            bri = jnp.stack([b_lru_r[li], b_lru_i[li]])
            consts = [w_in_odd[li].astype(BF16), w_out_odd[li].astype(BF16)] + common + [
                (rope, pl.BlockSpec((CHUNK, 2 * N_HEADS * DK), lambda j: (j, 0))), cw, wri, bri,
                lru_lambda[li][None, :], g_ret_norm[li][None, :]]
            states = [state_conv[li].astype(F32), state_rglru[li].astype(F32)[:, None, :], state_ret[li].astype(F32)]
            x1, xs, gs, dsl, meta, cv, hh, r = _mixer_call(_odd_kernel, "odd_mixer", first, head, consts, states, ODD_W,
                                                           extra_scratch=[pltpu.VMEM((8 + CHUNK, R_WIDTH), F32)])
            new["conv"].append(cv)
            new["h"].append(hh[:, 0, :])
            new["r"].append(r)
        n_used, tile_ea, tile_eb, wid = _moe_tables(meta)
        ys = _moe_call(n_used, tile_ea, tile_eb, wid, xs, gs, w_exp_gate[layer].astype(BF16),
                       w_exp_up[layer].astype(BF16), w_exp_down[layer].astype(BF16))
        head = [x1, ys, dsl, jnp.stack([ln2_g[layer], ln2_b[layer]])]

    y_prompt, y_sample = _final_call(*head)
    st = {k: jnp.stack(v) for k, v in new.items()}
    order = ("c", "n", "m", "g", "conv", "h", "r")
    dtypes = (state_mlstm_c.dtype, state_mlstm_n.dtype, state_mlstm_m.dtype, state_gla.dtype, state_conv.dtype,
              state_rglru.dtype, state_ret.dtype)
    prompt_states = tuple(st[k][:, :BATCH].astype(d) for k, d in zip(order, dtypes))
    sample_states = tuple(st[k][:, BATCH:].astype(d) for k, d in zip(order, dtypes))
    return (y_prompt.astype(x_prompt.dtype), y_sample.astype(x_sample.dtype)) + prompt_states + sample_states
```

```python
import functools
import math

import numpy as np
import jax
import jax.numpy as jnp
from jax import lax
from jax.experimental import pallas as pl
from jax.experimental.pallas import tpu as pltpu

F32 = jnp.float32
BF16 = jnp.bfloat16

D_MODEL = 1024
BATCH = 8
SEQ = 2048
DEPTH = 4
DEC_BATCH = 16
DEC_SEQ = 64
PAST_LEN = 1024
CHUNK = 64
HALF = D_MODEL // 2
N_HEADS = 4
DK = 64
DV = 128
G_RANK = 16
G_TAU = 16.0
R_WIDTH = HALF
R_BS = 64
CONV_W = 4
LRU_C = 8.0
ROPE_BASE = 10000.0
N_GROUPS = 4
EXPERTS_PER_GROUP = 4
N_EXPERTS = 16
D_EXPERT = D_MODEL // 4
DN_ALPHA = (2 * DEPTH) ** 0.25
LN_EPS = 1e-5
HEAD_EPS = 1e-6

SEQ_BLK = 8
ROW_BLK = SEQ_BLK * CHUNK
N_CHUNK = SEQ // CHUNK
N_SAMPLE_BLK = DEC_BATCH // SEQ_BLK
N_STEPS = N_CHUNK + N_SAMPLE_BLK
N_TOK = N_STEPS * ROW_BLK
N_STATE_SEQ = BATCH + DEC_BATCH

LANE = 128
EVEN_W = 3200
ODD_W = 2560
E_MQ, E_MK, E_MV, E_MO, E_GQ, E_GK, E_GV, E_GR, E_SM = 0, 256, 512, 1024, 1536, 1792, 2048, 2560, 3072
O_XB, O_GB, O_TQ, O_TK, O_TV, O_TG = 0, 512, 1024, 1280, 1536, 2048
GLA_LEVELS = (32, 16, 8, 4, 2, 1)
ROUTE_E0 = 16
PROJ_TILE = 640

PAIRS = ((0, 1), (0, 2), (0, 3), (1, 2), (1, 3), (2, 3))
N_CLASS = N_GROUPS * len(PAIRS)
WIN = 16
CAP_WIN = ROW_BLK // WIN + N_CLASS
CAP_ROWS = CAP_WIN * WIN
N_WIN = N_STEPS * CAP_WIN
WPT = 16
TILE_ROWS = WPT * WIN
N_TILES = -(-N_WIN // WPT) + N_CLASS
VMEM_LIMIT = 58 * 1024 * 1024


def _logsig(x):
    return jnp.minimum(x, 0.0) - jnp.log1p(jnp.exp(-jnp.abs(x)))


def _softplus(x):
    return jnp.maximum(x, 0.0) + jnp.log1p(jnp.exp(-jnp.abs(x)))


def _sigmoid(x):
    return 1.0 / (1.0 + jnp.exp(-x))


def _silu(x):
    return x * _sigmoid(x)


def _gelu_tanh(x):
    return 0.5 * x * (1.0 + jnp.tanh(math.sqrt(2.0 / math.pi) * (x + 0.044715 * (x * x * x))))


def _dot(a, b):
    return jnp.dot(a, b, preferred_element_type=F32)


def _dot_nt(a, b):
    return lax.dot_general(a, b, (((1,), (1,)), ((), ())), preferred_element_type=F32)


def _dot_tn(a, b):
    return lax.dot_general(a, b, (((0,), (0,)), ((), ())), preferred_element_type=F32)


def _split2(x):
    hi = x.astype(BF16)
    return hi, (x - hi.astype(F32)).astype(BF16)


def _split3(x):
    hi = x.astype(BF16)
    r1 = x - hi.astype(F32)
    mid = r1.astype(BF16)
    lo = (r1 - mid.astype(F32)).astype(BF16)
    return hi, mid, lo


def _bdot(a, b):
    return lax.dot_general(a, b, (((2,), (1,)), ((0,), (0,))), preferred_element_type=F32)


def _bdot_nt(a, b):
    return lax.dot_general(a, b, (((2,), (2,)), ((0,), (0,))), preferred_element_type=F32)


def _bdot_tn(a, b):
    return lax.dot_general(a, b, (((1,), (1,)), ((0,), (0,))), preferred_element_type=F32)


def _rows_bdot(mat01, parts):
    m = jnp.broadcast_to(mat01[None], (parts[0].shape[0],) + mat01.shape)
    return _bdot(m, parts[0]) + _bdot(m, parts[1]) + _bdot(m, parts[2])


def _head_norm(h, g):
    mu = jnp.mean(h, axis=-1, keepdims=True)
    d = h - mu
    var = jnp.mean(d * d, axis=-1, keepdims=True)
    return d * lax.rsqrt(var + HEAD_EPS) * g


def _layer_norm(x, g, b):
    mu = jnp.mean(x, axis=-1, keepdims=True)
    d = x - mu
    var = jnp.mean(d * d, axis=-1, keepdims=True)
    return d * lax.rsqrt(var + LN_EPS) * g + b


def _iota(shape, dim):
    return lax.broadcasted_iota(jnp.int32, shape, dim)


def _combine_ln(x1_ref, ys_ref, dsl_ref, ln_ref):
    dest = dsl_ref[:, 0:1]
    pos = _iota((ROW_BLK, CAP_ROWS), 1).astype(F32)
    sel = jnp.where(pos == dest, 1.0, 0.0).astype(BF16)
    moe = _dot(sel, ys_ref[...])
    return _layer_norm(DN_ALPHA * x1_ref[...] + moe, ln_ref[0:1, :], ln_ref[1:2, :])


def _load_x(first, head_refs, x_scr):
    j = pl.program_id(0)
    if first:
        xp_ref, xsm_ref = head_refs

        @pl.when(j < N_CHUNK)
        def _():
            x_scr[...] = xp_ref[...].reshape(ROW_BLK, D_MODEL)

        @pl.when(j >= N_CHUNK)
        def _():
            x_scr[...] = xsm_ref[...].reshape(ROW_BLK, D_MODEL)
    else:
        x_scr[...] = _combine_ln(*head_refs)


def _project_in(x_scr, win_ref, z_scr, width):
    xb = x_scr[...].astype(BF16)
    for c0 in range(0, width, PROJ_TILE):
        z_scr[:, :, c0:c0 + PROJ_TILE] = _dot(xb, win_ref[:, c0:c0 + PROJ_TILE]).reshape(SEQ_BLK, CHUNK, PROJ_TILE)


def _route(x1, xh, wr_ref, br_ref):
    xl = (x1 - xh.astype(F32)).astype(BF16)
    wh = wr_ref[:, 0:LANE]
    wl = wr_ref[:, LANE:2 * LANE]
    logits = _dot(xh, wh) + _dot(xh, wl) + _dot(xl, wh) + br_ref[...]
    lane = _iota(logits.shape, 1).astype(F32)
    neg = -jnp.inf
    big = float(4 * LANE)
    gl = jnp.where(lane < N_GROUPS, logits, neg)
    gmax = jnp.max(gl, axis=1, keepdims=True)
    gsum = jnp.sum(jnp.exp(gl - gmax), axis=1, keepdims=True)
    g_p = 1.0 / gsum
    g_idx = jnp.min(jnp.where(gl == gmax, lane, big), axis=1, keepdims=True)
    lo = ROUTE_E0 + EXPERTS_PER_GROUP * g_idx
    in_grp = (lane >= lo) & (lane < lo + EXPERTS_PER_GROUP)
    el = jnp.where(in_grp, logits, neg)
    e1 = jnp.max(el, axis=1, keepdims=True)
    i1 = jnp.min(jnp.where(el == e1, lane, big), axis=1, keepdims=True)
    el2 = jnp.where(lane == i1, neg, el)
    e2 = jnp.max(el2, axis=1, keepdims=True)
    i2 = jnp.min(jnp.where(el2 == e2, lane, big), axis=1, keepdims=True)
    t = jnp.exp(e2 - e1)
    w1 = g_p / (1.0 + t)
    w2 = g_p * t / (1.0 + t)
    first_lower = i1 < i2
    la = jnp.minimum(i1, i2) - lo
    lb = jnp.maximum(i1, i2) - lo
    pair = la * (7.0 - la) * 0.5 + (lb - la - 1.0)
    cls = g_idx * float(len(PAIRS)) + pair
    return cls, jnp.where(first_lower, w1, w2), jnp.where(first_lower, w2, w1)


def _mixer_tail(x_scr, y_scr, wout_ref, ln_ref, wr_ref, br_ref, lt_ref, ut_ref,
                x1_ref, xs_ref, gs_ref, dsl_ref, meta_ref):
    y = _dot(y_scr[...].reshape(ROW_BLK, D_MODEL).astype(BF16), wout_ref[...])
    x1 = _layer_norm(DN_ALPHA * x_scr[...] + y, ln_ref[0:1, :], ln_ref[1:2, :])
    x1_ref[...] = x1
    x1b = x1.astype(BF16)
    cls, g_a, g_b = _route(x1, x1b, wr_ref, br_ref)

    lane = _iota((ROW_BLK, LANE), 1).astype(F32)
    onehot = jnp.where(lane == cls, 1.0, 0.0)
    earlier = _dot(lt_ref[...], onehot.astype(BF16))
    rank = jnp.sum(earlier * onehot, axis=1, keepdims=True)
    cnt = jnp.sum(onehot, axis=0, keepdims=True)
    nwin = jnp.floor((cnt + (WIN - 1.0)) * (1.0 / WIN))
    woff = _dot(jnp.broadcast_to(nwin, (8, LANE)).astype(BF16), ut_ref[...])[0:1]
    dest = WIN * jnp.sum(onehot * woff, axis=1, keepdims=True) + rank
    pos = _iota((ROW_BLK, CAP_ROWS), 1).astype(F32)
    sel = jnp.where(pos == dest, 1.0, 0.0).astype(BF16)
    for c0 in range(0, D_MODEL, 256):
        xs_ref[:, c0:c0 + 256] = _dot_tn(sel, x1b[:, c0:c0 + 256]).astype(BF16)
    ga_hi, ga_lo = _split2(g_a)
    gb_hi, gb_lo = _split2(g_b)
    gpack = jnp.where(lane == 0, ga_hi.astype(F32), jnp.where(lane == 1, ga_lo.astype(F32),
            jnp.where(lane == 2, gb_hi.astype(F32), jnp.where(lane == 3, gb_lo.astype(F32), 0.0))))
    gs_ref[...] = _dot_tn(sel, gpack.astype(BF16))
    dsl_ref[...] = jnp.where(lane == 0, dest, 0.0)
    meta_ref[...] = jnp.concatenate([cnt, nwin, woff, jnp.zeros((5, LANE), F32)], axis=0)


def _init_states(j, pairs):
    @pl.when(j == 0)
    def _():
        for _, out_ref in pairs:
            out_ref[...] = jnp.zeros(out_ref.shape, out_ref.dtype)

    @pl.when(j >= N_CHUNK)
    def _():
        for in_ref, out_ref in pairs:
            out_ref[...] = in_ref[...]


def _even_kernel(*refs, first):
    nh = 2 if first else 4
    head_refs = refs[:nh]
    (win_ref, wout_ref, ln_ref, wr_ref, br_ref, lt_ref, ut_ref, cm_ref, bsm_ref, gm_ref, wa2_ref, ba_ref, gg_ref,
     c_in, n_in, m_in, s_in,
     x1_ref, xs_ref, gs_ref, dsl_ref, meta_ref, c_out, n_out, m_out, s_out,
     x_scr, z_scr, y_scr) = refs[nh:]
    j = pl.program_id(0)
    _init_states(j, ((c_in, c_out), (n_in, n_out), (m_in, m_out), (s_in, s_out)))
    _load_x(first, head_refs, x_scr)
    _project_in(x_scr, win_ref, z_scr, EVEN_W)

    rr = _iota((1, CHUNK, CHUNK), 1)
    cc = _iota((1, CHUNK, CHUNK), 2)
    causal = cc <= rr
    eye = cc == rr
    trow = _iota((1, CHUNK, 1), 1)
    tri = cm_ref[0:CHUNK, :]
    neg = -jnp.inf

    n_all = n_out[...]
    m_all = m_out[...]
    lane_m = _iota((1, 1, LANE), 2)
    n_new = []
    m_new = m_all
    slab = z_scr[:, :, E_SM:E_SM + LANE]
    pre = slab + bsm_ref[...]
    fcum = _rows_bdot(tri, _split3(_logsig(pre)))
    at = jnp.swapaxes(jnp.concatenate([pre, fcum], axis=1), 1, 2)

    for h in range(N_HEADS):
        bc = fcum[:, :, N_HEADS + h:N_HEADS + h + 1]
        icol = pre[:, :, h:h + 1]
        br = at[:, N_HEADS + h:N_HEADS + h + 1, CHUNK:2 * CHUNK]
        ir = at[:, h:h + 1, 0:CHUNK]
        mprev = m_all[:, :, h:h + 1]
        logd = jnp.where(causal, bc - br + ir, neg)
        linter = bc + mprev
        mrow = jnp.maximum(linter, jnp.max(logd, axis=2, keepdims=True))
        wintra = jnp.exp(logd - mrow)
        winter = jnp.exp(linter - mrow)
        qh = z_scr[:, :, E_MQ + h * DK:E_MQ + (h + 1) * DK]
        kh = z_scr[:, :, E_MK + h * DK:E_MK + (h + 1) * DK] * (DK ** -0.5)
        vb = z_scr[:, :, E_MV + h * DV:E_MV + (h + 1) * DV].astype(BF16)
        qb = qh.astype(BF16)
        sm = _bdot_nt(qb, kh.astype(BF16)) * wintra
        cst = c_out[:, h]
        nrow = n_all[:, :, h * DK:(h + 1) * DK]
        num = _bdot(sm.astype(BF16), vb) + winter * _bdot(qb, cst.astype(BF16))
        den = jnp.sum(sm, axis=2, keepdims=True) + winter * jnp.sum(qh * nrow, axis=2, keepdims=True)
        hh = num / jnp.maximum(jnp.abs(den), jnp.exp(-mrow))
        blast = bc[:, CHUNK - 1:CHUNK, :]
        logw = blast - bc + icol
        mnew = jnp.maximum(blast + mprev, jnp.max(logw, axis=1, keepdims=True))
        kw = kh * jnp.exp(logw - mnew)
        decay = jnp.exp(blast + mprev - mnew)
        c_out[:, h] = decay * cst + _bdot_tn(kw.astype(BF16), vb)
        n_new.append(decay * nrow + jnp.sum(kw, axis=1, keepdims=True))
        m_new = jnp.where(lane_m == h, mnew, m_new)
        mo = z_scr[:, :, E_MO + h * DV:E_MO + (h + 1) * DV]
        y_scr[:, :, h * DV:(h + 1) * DV] = _sigmoid(mo) * _head_norm(hh, gm_ref[:, h * DV:(h + 1) * DV])

    n_out[...] = jnp.concatenate(n_new, axis=2)
    m_out[...] = m_new

    la = _logsig(_dot(slab.reshape(ROW_BLK, LANE).astype(BF16), wa2_ref[...]) + ba_ref[...]) * (1.0 / G_TAU)
    la = la.reshape(SEQ_BLK, CHUNK, N_HEADS * DK)
    la_parts = _split3(la)
    bg = _rows_bdot(tri, la_parts)
    gq = z_scr[:, :, E_GQ:E_GQ + N_HEADS * DK] * (DK ** -0.5)
    gk = z_scr[:, :, E_GK:E_GK + N_HEADS * DK]
    blast = bg[:, CHUNK - 1:CHUNK, :]
    qdec = (gq * jnp.exp(bg)).astype(BF16)
    kdec = (gk * jnp.exp(blast - bg)).astype(BF16)
    lat = jnp.swapaxes(jnp.concatenate([la, jnp.zeros_like(la)], axis=1), 1, 2)
    sdec = jnp.exp(jnp.sum(lat, axis=2, keepdims=True))
    gqb = gq.astype(BF16)
    gkb = gk.astype(BF16)
    qts, kts = [], []
    for li, m in enumerate(GLA_LEVELS):
        r0 = CHUNK * (1 + 2 * li)
        ex = _rows_bdot(cm_ref[r0:r0 + 2 * CHUNK, :], la_parts)
        up = (trow & (2 * m - 1)) >= m
        qts.append(jnp.where(up, gq * jnp.exp(ex[:, 0:CHUNK]), 0.0).astype(BF16))
        kts.append(jnp.where(up, 0.0, gk * jnp.exp(ex[:, CHUNK:2 * CHUNK])).astype(BF16))
    for h in range(N_HEADS):
        hs = slice(h * DK, (h + 1) * DK)
        att = jnp.where(eye, _bdot_nt(gqb[:, :, hs], gkb[:, :, hs]), 0.0)
        for li, m in enumerate(GLA_LEVELS):
            sh = int(math.log2(2 * m))
            same = (rr >> sh) == (cc >> sh)
            att = att + jnp.where(same, _bdot_nt(qts[li][:, :, hs], kts[li][:, :, hs]), 0.0)
        vb = z_scr[:, :, E_GV + h * DV:E_GV + (h + 1) * DV].astype(BF16)
        sst = s_out[:, h]
        og = _bdot(att.astype(BF16), vb) + _bdot(qdec[:, :, hs], sst.astype(BF16))
        s_out[:, h] = sdec[:, h * DK:(h + 1) * DK, :] * sst + _bdot_tn(kdec[:, :, hs], vb)
        gr = z_scr[:, :, E_GR + h * DV:E_GR + (h + 1) * DV]
        y_scr[:, :, HALF + h * DV:HALF + (h + 1) * DV] = _silu(gr) * _head_norm(og, gg_ref[:, h * DV:(h + 1) * DV])

    _mixer_tail(x_scr, y_scr, wout_ref, ln_ref, wr_ref, br_ref, lt_ref, ut_ref,
                x1_ref, xs_ref, gs_ref, dsl_ref, meta_ref)


def _odd_kernel(*refs, first):
    nh = 2 if first else 4
    head_refs = refs[:nh]
    (win_ref, wout_ref, ln_ref, wr_ref, br_ref, lt_ref, ut_ref, rope_ref, cw_ref, wri_ref, bri_ref, lam_ref, gt_ref,
     cv_in, h_in, r_in,
     x1_ref, xs_ref, gs_ref, dsl_ref, meta_ref, cv_out, h_out, r_out,
     x_scr, z_scr, y_scr, pad_scr) = refs[nh:]
    j = pl.program_id(0)
    _init_states(j, ((cv_in, cv_out), (h_in, h_out), (r_in, r_out)))
    _load_x(first, head_refs, x_scr)
    _project_in(x_scr, win_ref, z_scr, ODD_W)

    rr = _iota((1, CHUNK, CHUNK), 1)
    cc = _iota((1, CHUNK, CHUNK), 2)
    causal = cc <= rr
    dist = (rr - cc).astype(F32)
    tf = _iota((1, CHUNK, 1), 1).astype(F32)
    trow = _iota((ROW_BLK, 1), 0) & (CHUNK - 1)
    lane4 = _iota((ROW_BLK, N_HEADS * DK), 1)
    first_half = (lane4 & (DK - 1)) < (DK // 2)
    lgs = [math.log1p(-(2.0 ** (-5.0 - h))) for h in range(N_HEADS)]
    sp = _softplus(-lam_ref[...])

    def flat(a3):
        return a3.reshape(ROW_BLK, a3.shape[-1])

    def streams(a2):
        return a2.reshape(SEQ_BLK, CHUNK, a2.shape[-1])

    cosf = flat(jnp.broadcast_to(rope_ref[:, 0:N_HEADS * DK][None], (SEQ_BLK, CHUNK, N_HEADS * DK)))
    sinf = flat(jnp.broadcast_to(rope_ref[:, N_HEADS * DK:2 * N_HEADS * DK][None], (SEQ_BLK, CHUNK, N_HEADS * DK)))

    def rot(a):
        swapped = jnp.where(first_half, pltpu.roll(a, N_HEADS * DK - DK // 2, 1), pltpu.roll(a, DK // 2, 1))
        return a * cosf + swapped * sinf

    xb = z_scr[:, :, O_XB:O_XB + R_WIDTH]
    pad_scr[:, 8 - (CONV_W - 1):8, :] = cv_out[...]
    pad_scr[:, 8:8 + CHUNK, :] = xb
    xc = cw_ref[CONV_W:CONV_W + 1, :] + xb * cw_ref[CONV_W - 1:CONV_W, :]
    for d in range(1, CONV_W):
        xc = xc + pad_scr[:, 8 - d:8 - d + CHUNK, :] * cw_ref[CONV_W - 1 - d:CONV_W - d, :]
    cv_out[...] = xb[:, CHUNK - (CONV_W - 1):CHUNK, :]
    xc = flat(xc)
    xcb = xc.astype(BF16)
    halves = []
    for t in range(2):
        halves.append(_dot(xcb[:, t * 256:(t + 1) * 256], wri_ref[t]))
    r_pre = jnp.concatenate([halves[0][:, 0:256], halves[1][:, 0:256]], axis=1) + bri_ref[0:1, :]
    i_pre = jnp.concatenate([halves[0][:, 256:512], halves[1][:, 256:512]], axis=1) + bri_ref[1:2, :]
    log_a = (-LRU_C) * _sigmoid(r_pre) * sp
    a = jnp.exp(log_a)
    th = jnp.tanh(log_a)
    u = jnp.sqrt((-2.0 * th) / (1.0 - th)) * (_sigmoid(i_pre) * xc)
    h0 = flat(jnp.broadcast_to(h_out[...], (SEQ_BLK, CHUNK, R_WIDTH)))
    u = u + jnp.where(trow == 0, a * h0, 0.0)
    for sh in (1, 2, 4, 8, 16, 32):
        valid = trow >= sh
        u_prev = pltpu.roll(u, sh, 0)
        a_prev = pltpu.roll(a, sh, 0)
        u = jnp.where(valid, a * u_prev + u, u)
        a = jnp.where(valid, a * a_prev, a)
    h_out[...] = streams(u)[:, CHUNK - 1:CHUNK, :]
    y_scr[:, :, 0:R_WIDTH] = streams(_gelu_tanh(flat(z_scr[:, :, O_GB:O_GB + R_WIDTH])) * u)

    q = streams(rot(flat(z_scr[:, :, O_TQ:O_TQ + N_HEADS * DK])).astype(BF16))
    kf = streams(rot(flat(z_scr[:, :, O_TK:O_TK + N_HEADS * DK])) * (DK ** -0.5))
    for h in range(N_HEADS):
        hs = slice(h * DK, (h + 1) * DK)
        lg = lgs[h]
        dec = jnp.where(causal, jnp.exp(dist * lg), 0.0)
        w_inter = jnp.exp((tf + 1.0) * lg)
        w_key = jnp.exp((CHUNK - 1.0 - tf) * lg)
        s_decay = math.exp(CHUNK * lg)
        qh = q[:, :, hs]
        kh = kf[:, :, hs]
        vb = z_scr[:, :, O_TV + h * DV:O_TV + (h + 1) * DV].astype(BF16)
        sst = r_out[:, h]
        att = _bdot_nt(qh, kh.astype(BF16)) * dec
        o = _bdot(att.astype(BF16), vb) + w_inter * _bdot(qh, sst.astype(BF16))
        r_out[:, h] = s_decay * sst + _bdot_tn((kh * w_key).astype(BF16), vb)
        tg = z_scr[:, :, O_TG + h * DV:O_TG + (h + 1) * DV]
        y_scr[:, :, HALF + h * DV:HALF + (h + 1) * DV] = _silu(tg) * _head_norm(o, gt_ref[:, h * DV:(h + 1) * DV])

    _mixer_tail(x_scr, y_scr, wout_ref, ln_ref, wr_ref, br_ref, lt_ref, ut_ref,
                x1_ref, xs_ref, gs_ref, dsl_ref, meta_ref)


def _moe_kernel(nused_ref, ea_ref, eb_ref, wid_ref, xs_hbm, gs_hbm, wga_ref, wua_ref, wda_ref, wgb_ref, wub_ref,
                wdb_ref, ys_hbm, xbuf, gbuf, ybuf, sem_x, sem_g, sem_y):
    i = pl.program_id(0)
    n_used = nused_ref[0]
    slot = i % 2

    def x_copy(w, sl, k):
        return pltpu.make_async_copy(xs_hbm.at[w], xbuf.at[sl, k], sem_x.at[sl, k])

    def g_copy(w, sl, k):
        return pltpu.make_async_copy(gs_hbm.at[w], gbuf.at[sl, k], sem_g.at[sl, k])

    def y_copy(w, sl, k):
        return pltpu.make_async_copy(ybuf.at[sl, k], ys_hbm.at[w], sem_y.at[sl, k])

    def for_windows(tile, fn):
        for k in range(WPT):
            w = wid_ref[tile * WPT + k]

            @pl.when(w >= 0)
            def _():
                fn(w, k)

    def start_gather(tile, sl):
        def fn(w, k):
            x_copy(w, sl, k).start()
            g_copy(w, sl, k).start()
        for_windows(tile, fn)

    @pl.when(i == 0)
    def _():
        xbuf[...] = jnp.zeros(xbuf.shape, xbuf.dtype)
        gbuf[...] = jnp.zeros(gbuf.shape, gbuf.dtype)
        start_gather(0, 0)

    @pl.when(i + 1 < n_used)
    def _():
        start_gather(i + 1, 1 - slot)

    @pl.when(i < n_used)
    def _():
        def wait_in(w, k):
            x_copy(w, slot, k).wait()
            g_copy(w, slot, k).wait()
        for_windows(i, wait_in)

        @pl.when(i >= 2)
        def _():
            for_windows(i - 2, lambda w, k: y_copy(w, slot, k).wait())

        x = xbuf[slot].reshape(TILE_ROWS, D_MODEL)
        g = gbuf[slot].reshape(TILE_ROWS, LANE)
        g_a = g[:, 0:1] + g[:, 1:2]
        g_b = g[:, 2:3] + g[:, 3:4]
        h_a = _silu(_dot(x, wga_ref[0])) * _dot(x, wua_ref[0]) * g_a
        h_b = _silu(_dot(x, wgb_ref[0])) * _dot(x, wub_ref[0]) * g_b
        y = _dot(h_a.astype(BF16), wda_ref[0]) + _dot(h_b.astype(BF16), wdb_ref[0])
        ybuf[slot] = y.astype(BF16).reshape(WPT, WIN, D_MODEL)
        for_windows(i, lambda w, k: y_copy(w, slot, k).start())

        @pl.when(i == n_used - 1)
        def _():
            for_windows(i, lambda w, k: y_copy(w, slot, k).wait())

            @pl.when(i >= 1)
            def _():
                for_windows(i - 1, lambda w, k: y_copy(w, 1 - slot, k).wait())


def _final_kernel(x1_ref, ys_ref, dsl_ref, ln_ref, yp_ref, ysm_ref):
    j = pl.program_id(0)
    x2 = _combine_ln(x1_ref, ys_ref, dsl_ref, ln_ref).reshape(SEQ_BLK, CHUNK, D_MODEL)

    @pl.when(j < N_CHUNK)
    def _():
        yp_ref[...] = x2

    @pl.when(j >= N_CHUNK)
    def _():
        ysm_ref[...] = x2


def _const_spec(shape):
    nd = len(shape)
    return pl.BlockSpec(shape, lambda j: (0,) * nd, pipeline_mode=pl.Buffered(1))


def _row_spec(rows, width):
    return pl.BlockSpec((rows, width), lambda j: (j, 0))


def _prompt_spec():
    return pl.BlockSpec((SEQ_BLK, CHUNK, D_MODEL), lambda j: (0, jnp.minimum(j, N_CHUNK - 1), 0))


def _sample_spec():
    return pl.BlockSpec((SEQ_BLK, CHUNK, D_MODEL), lambda j: (jnp.maximum(j - N_CHUNK, 0), 0, 0))


def _state_in_spec(shape):
    nd = len(shape)
    return pl.BlockSpec((SEQ_BLK,) + shape[1:], lambda j: (jnp.maximum(j - N_CHUNK, 0),) + (0,) * (nd - 1))


def _state_out_spec(shape):
    nd = len(shape)
    return pl.BlockSpec((SEQ_BLK,) + shape[1:], lambda j: (jnp.maximum(j - N_CHUNK + 1, 0),) + (0,) * (nd - 1))


def _head_specs(first):
    if first:
        return [_prompt_spec(), _sample_spec()]
    return [_row_spec(ROW_BLK, D_MODEL), _row_spec(CAP_ROWS, D_MODEL), _row_spec(ROW_BLK, LANE),
            _const_spec((2, D_MODEL))]


def _mixer_call(kernel_fn, name, first, head, consts, states, width, extra_scratch=()):
    const_specs = [c[1] if isinstance(c, tuple) else _const_spec(c.shape) for c in consts]
    consts = [c[0] if isinstance(c, tuple) else c for c in consts]
    state_specs = [_state_in_spec(s.shape) for s in states]
    out_shapes = [jax.ShapeDtypeStruct((N_TOK, D_MODEL), F32),
                  jax.ShapeDtypeStruct((N_STEPS * CAP_ROWS, D_MODEL), BF16),
                  jax.ShapeDtypeStruct((N_STEPS * CAP_ROWS, LANE), F32),
                  jax.ShapeDtypeStruct((N_TOK, LANE), F32),
                  jax.ShapeDtypeStruct((N_STEPS * 8, LANE), F32)]
    out_specs = [_row_spec(ROW_BLK, D_MODEL), _row_spec(CAP_ROWS, D_MODEL), _row_spec(CAP_ROWS, LANE),
                 _row_spec(ROW_BLK, LANE), _row_spec(8, LANE)]
    for s in states:
        shp = (N_STATE_SEQ,) + s.shape[1:]
        out_shapes.append(jax.ShapeDtypeStruct(shp, F32))
        out_specs.append(_state_out_spec(shp))
    return pl.pallas_call(
        functools.partial(kernel_fn, first=first),
        grid=(N_STEPS,),
        in_specs=_head_specs(first) + const_specs + state_specs,
        out_specs=out_specs,
        out_shape=out_shapes,
        scratch_shapes=[pltpu.VMEM((ROW_BLK, D_MODEL), F32), pltpu.VMEM((SEQ_BLK, CHUNK, width), F32),
                        pltpu.VMEM((SEQ_BLK, CHUNK, D_MODEL), F32)] + list(extra_scratch),
        compiler_params=pltpu.CompilerParams(dimension_semantics=("arbitrary",), vmem_limit_bytes=VMEM_LIMIT),
        name=name,
    )(*head, *consts, *states)


def _moe_call(n_used, tile_ea, tile_eb, wid, xs, gs, wg, wu, wd):
    xs_w = xs.reshape(N_WIN, WIN, D_MODEL)
    gs_w = gs.reshape(N_WIN, WIN, LANE)
    wspec = lambda shape, sel: pl.BlockSpec((1,) + shape, lambda i, nu, ea, eb, wi: ((ea, eb)[sel][i], 0, 0))
    grid_spec = pltpu.PrefetchScalarGridSpec(
        num_scalar_prefetch=4,
        grid=(N_TILES,),
        in_specs=[pl.BlockSpec(memory_space=pl.ANY), pl.BlockSpec(memory_space=pl.ANY),
                  wspec((D_MODEL, D_EXPERT), 0), wspec((D_MODEL, D_EXPERT), 0), wspec((D_EXPERT, D_MODEL), 0),
                  wspec((D_MODEL, D_EXPERT), 1), wspec((D_MODEL, D_EXPERT), 1), wspec((D_EXPERT, D_MODEL), 1)],
        out_specs=pl.BlockSpec(memory_space=pl.ANY),
        scratch_shapes=[pltpu.VMEM((2, WPT, WIN, D_MODEL), BF16), pltpu.VMEM((2, WPT, WIN, LANE), F32),
                        pltpu.VMEM((2, WPT, WIN, D_MODEL), BF16),
                        pltpu.SemaphoreType.DMA((2, WPT)), pltpu.SemaphoreType.DMA((2, WPT)),
                        pltpu.SemaphoreType.DMA((2, WPT))])
    ys_w = pl.pallas_call(
        _moe_kernel,
        grid_spec=grid_spec,
        out_shape=jax.ShapeDtypeStruct((N_WIN, WIN, D_MODEL), BF16),
        input_output_aliases={4: 0},
        compiler_params=pltpu.CompilerParams(dimension_semantics=("arbitrary",), vmem_limit_bytes=VMEM_LIMIT),
        name="moe",
    )(n_used, tile_ea, tile_eb, wid, xs_w, gs_w, wg, wu, wd, wg, wu, wd)
    return ys_w.reshape(N_STEPS * CAP_ROWS, D_MODEL)


def _final_call(x1, ys, dsl, ln):
    return pl.pallas_call(
        _final_kernel,
        grid=(N_STEPS,),
        in_specs=[_row_spec(ROW_BLK, D_MODEL), _row_spec(CAP_ROWS, D_MODEL), _row_spec(ROW_BLK, LANE),
                  _const_spec((2, D_MODEL))],
        out_specs=[_prompt_spec(), _sample_spec()],
        out_shape=[jax.ShapeDtypeStruct((BATCH, SEQ, D_MODEL), F32),
                   jax.ShapeDtypeStruct((DEC_BATCH, DEC_SEQ, D_MODEL), F32)],
        compiler_params=pltpu.CompilerParams(dimension_semantics=("arbitrary",), vmem_limit_bytes=VMEM_LIMIT),
        name="final_ln",
    )(x1, ys, dsl, ln)


def _moe_tables(meta):
    i32 = jnp.int32
    meta = meta.reshape(N_STEPS, 8, LANE)
    nwin = meta[:, 1, :N_CLASS].astype(i32)
    woff = meta[:, 2, :N_CLASS].astype(i32)
    cum_incl = jnp.cumsum(nwin, axis=0)
    cum_excl = cum_incl - nwin
    per_class = cum_incl[-1]
    tiles_per_class = (per_class + WPT - 1) // WPT
    tile_end = jnp.cumsum(tiles_per_class)
    tile_start = tile_end - tiles_per_class
    n_used = tile_end[-1]
    tile = jnp.arange(N_TILES, dtype=i32)
    tile_cls = jnp.minimum(jnp.sum((tile[:, None] >= tile_end[None, :]).astype(i32), axis=1), N_CLASS - 1)
    last_cls = tile_cls[jnp.maximum(n_used - 1, 0)]
    tile_cls = jnp.where(tile < n_used, tile_cls, last_cls)
    ordinal = (tile - tile_start[tile_cls])[:, None] * WPT + jnp.arange(WPT, dtype=i32)[None, :]
    valid = (tile[:, None] < n_used) & (ordinal < per_class[tile_cls][:, None])
    cum_c = cum_incl.T[tile_cls]
    blk = jnp.sum((ordinal[:, :, None] >= cum_c[:, None, :]).astype(i32), axis=2)
    blk = jnp.minimum(blk, N_STEPS - 1)
    cls2 = jnp.broadcast_to(tile_cls[:, None], blk.shape)
    wid = blk * CAP_WIN + woff[blk, cls2] + ordinal - cum_excl[blk, cls2]
    wid = jnp.where(valid, wid, -1).astype(i32).reshape(-1)
    pair_a = jnp.asarray([p[0] for p in PAIRS], i32)
    pair_b = jnp.asarray([p[1] for p in PAIRS], i32)
    grp = tile_cls // len(PAIRS)
    tile_ea = grp * EXPERTS_PER_GROUP + pair_a[tile_cls % len(PAIRS)]
    tile_eb = grp * EXPERTS_PER_GROUP + pair_b[tile_cls % len(PAIRS)]
    return n_used.reshape(1).astype(i32), tile_ea.astype(i32), tile_eb.astype(i32), wid


def _gla_sum_matrix():
    t = np.arange(CHUNK)
    mats = [(t[None, :] <= t[:, None])]
    for m in GLA_LEVELS:
        mid = (t // (2 * m)) * (2 * m) + m
        upper = (t % (2 * m)) >= m
        up = upper[:, None] & (t[None, :] > mid[:, None]) & (t[None, :] <= t[:, None])
        lo = (~upper)[:, None] & (t[None, :] > t[:, None]) & (t[None, :] <= mid[:, None])
        mats += [up, lo]
    return jnp.asarray(np.concatenate(mats, axis=0).astype(np.float32), dtype=BF16)


def _rope_table():
    inv = ROPE_BASE ** (-jnp.arange(0, DK, 2, dtype=F32) / DK)
    pos = jnp.concatenate([jnp.arange(SEQ, dtype=F32)] + [PAST_LEN + jnp.arange(DEC_SEQ, dtype=F32)] * N_SAMPLE_BLK)
    ang = pos[:, None] * inv[None]
    cos, sin = jnp.cos(ang), jnp.sin(ang)
    cosf = jnp.tile(jnp.concatenate([cos, cos], axis=1), (1, N_HEADS))
    sinf = jnp.tile(jnp.concatenate([-sin, sin], axis=1), (1, N_HEADS))
    return jnp.concatenate([cosf, sinf], axis=1)


def _pad_lanes(a, width=LANE):
    return jnp.pad(a, [(0, 0)] * (a.ndim - 1) + [(0, width - a.shape[-1])])


def _router_pack(w_grp, b_grp, w_rt, b_rt):
    w = jnp.zeros((D_MODEL, LANE), F32).at[:, 0:N_GROUPS].set(w_grp).at[:, ROUTE_E0:ROUTE_E0 + N_EXPERTS].set(w_rt)
    wh = w.astype(BF16)
    wl = (w - wh.astype(F32)).astype(BF16)
    b = jnp.zeros((1, LANE), F32).at[0, 0:N_GROUPS].set(b_grp).at[0, ROUTE_E0:ROUTE_E0 + N_EXPERTS].set(b_rt)
    return jnp.concatenate([wh, wl], axis=1), b


def kernel(x_prompt, x_sample, state_mlstm_c, state_mlstm_n, state_mlstm_m, state_gla, state_conv, state_rglru, state_ret, w_in_even, b_mlstm_i, b_mlstm_f, g_mlstm_norm, w_gla_a2, b_gla_a, g_gla_norm, w_out_even, w_in_odd, w_conv, b_conv, w_lru_r, b_lru_r, w_lru_i, b_lru_i, lru_lambda, g_ret_norm, w_out_odd, ln1_g, ln1_b, ln2_g, ln2_b, w_router_group, b_router_group, w_router_expert, b_router_expert, w_exp_gate, w_exp_up, w_exp_down):
    cm = _gla_sum_matrix()
    rope = _rope_table()
    tt = np.arange(ROW_BLK)
    lower_strict = jnp.asarray((tt[None, :] < tt[:, None]).astype(np.float32), dtype=BF16)
    ll = np.arange(LANE)
    upper_strict = jnp.asarray((ll[:, None] < ll[None, :]).astype(np.float32), dtype=BF16)

    head = [x_prompt.astype(F32), x_sample.astype(F32)]
    new = {k: [] for k in ("c", "n", "m", "g", "conv", "h", "r")}
    for layer in range(DEPTH):
        li = layer // 2
        first = layer == 0
        wr, br = _router_pack(w_router_group[layer], b_router_group[layer], w_router_expert[layer],
                              b_router_expert[layer])
        ln1 = jnp.stack([ln1_g[layer], ln1_b[layer]])
        common = [ln1, wr, br, lower_strict, upper_strict]
        if layer % 2 == 0:
            w = w_in_even[li]
            win = jnp.concatenate([w[:, 0:1024], w[:, 1032:1544], w[:, 1544:2568], w[:, 2584:3096], w[:, 1024:1032],
                                   w[:, 2568:2584], jnp.zeros((D_MODEL, EVEN_W - 3096), w.dtype)], axis=1).astype(BF16)
            bsm = jnp.zeros((1, LANE), F32).at[0, 0:N_HEADS].set(b_mlstm_i[li]).at[0, N_HEADS:2 * N_HEADS].set(b_mlstm_f[li])
            wa2 = jnp.zeros((LANE, N_HEADS * DK), F32).at[2 * N_HEADS:2 * N_HEADS + G_RANK].set(w_gla_a2[li]).astype(BF16)
            consts = [win, w_out_even[li].astype(BF16)] + common + [cm, bsm, g_mlstm_norm[li][None, :], wa2,
                                                                     b_gla_a[li][None, :], g_gla_norm[li][None, :]]
            states = [state_mlstm_c[li].astype(F32), state_mlstm_n[li].astype(F32).reshape(DEC_BATCH, 1, N_HEADS * DK),
                      _pad_lanes(state_mlstm_m[li].astype(F32))[:, None, :], state_gla[li].astype(F32)]
            x1, xs, gs, dsl, meta, c, n, m, g = _mixer_call(_even_kernel, "even_mixer", first, head, consts, states,
                                                            EVEN_W)
            new["c"].append(c)
            new["n"].append(n.reshape(N_STATE_SEQ, N_HEADS, DK))
            new["m"].append(m[:, 0, 0:N_HEADS])
            new["g"].append(g)
        else:
            cw = jnp.concatenate([w_conv[li], b_conv[li][None, :]], axis=0)
            wr4 = w_lru_r[li].reshape(2, 4, R_BS, R_BS)
            wi4 = w_lru_i[li].reshape(2, 4, R_BS, R_BS)
            eye4 = jnp.eye(4, dtype=F32)
            bd = lambda w4: jnp.einsum("tncd,nm->tncmd", w4, eye4).reshape(2, 256, 256)
            wri = jnp.concatenate([bd(wr4), bd(wi4)], axis=2).astype(BF16)
            bri = jnp.stack([b_lru_r[li], b_lru_i[li]])
            consts = [w_in_odd[li].astype(BF16), w_out_odd[li].astype(BF16)] + common + [
                (rope, pl.BlockSpec((CHUNK, 2 * N_HEADS * DK), lambda j: (j, 0))), cw, wri, bri,
                lru_lambda[li][None, :], g_ret_norm[li][None, :]]
            states = [state_conv[li].astype(F32), state_rglru[li].astype(F32)[:, None, :], state_ret[li].astype(F32)]
            x1, xs, gs, dsl, meta, cv, hh, r = _mixer_call(_odd_kernel, "odd_mixer", first, head, consts, states, ODD_W,
                                                           extra_scratch=[pltpu.VMEM((SEQ_BLK, 8 + CHUNK, R_WIDTH), F32)])
            new["conv"].append(cv)
            new["h"].append(hh[:, 0, :])
            new["r"].append(r)
        n_used, tile_ea, tile_eb, wid = _moe_tables(meta)
        ys = _moe_call(n_used, tile_ea, tile_eb, wid, xs, gs, w_exp_gate[layer].astype(BF16),
                       w_exp_up[layer].astype(BF16), w_exp_down[layer].astype(BF16))
        head = [x1, ys, dsl, jnp.stack([ln2_g[layer], ln2_b[layer]])]

    y_prompt, y_sample = _final_call(*head)
    st = {k: jnp.stack(v) for k, v in new.items()}
    order = ("c", "n", "m", "g", "conv", "h", "r")
    dtypes = (state_mlstm_c.dtype, state_mlstm_n.dtype, state_mlstm_m.dtype, state_gla.dtype, state_conv.dtype,
              state_rglru.dtype, state_ret.dtype)
    prompt_states = tuple(st[k][:, :BATCH].astype(d) for k, d in zip(order, dtypes))
    sample_states = tuple(st[k][:, BATCH:].astype(d) for k, d in zip(order, dtypes))
    return (y_prompt.astype(x_prompt.dtype), y_sample.astype(x_sample.dtype)) + prompt_states + sample_states
```

```python
import functools
import math

import numpy as np
import jax
import jax.numpy as jnp
from jax import lax
from jax.experimental import pallas as pl
from jax.experimental.pallas import tpu as pltpu

F32 = jnp.float32
BF16 = jnp.bfloat16

D_MODEL = 1024
BATCH = 8
SEQ = 2048
DEPTH = 4
DEC_BATCH = 16
DEC_SEQ = 64
PAST_LEN = 1024
CHUNK = 64
HALF = D_MODEL // 2
N_HEADS = 4
DK = 64
DV = 128
G_RANK = 16
G_TAU = 16.0
R_WIDTH = HALF
R_BS = 64
CONV_W = 4
LRU_C = 8.0
ROPE_BASE = 10000.0
N_GROUPS = 4
EXPERTS_PER_GROUP = 4
N_EXPERTS = 16
D_EXPERT = D_MODEL // 4
DN_ALPHA = (2 * DEPTH) ** 0.25
LN_EPS = 1e-5
HEAD_EPS = 1e-6

SEQ_BLK = 8
ROW_BLK = SEQ_BLK * CHUNK
N_CHUNK = SEQ // CHUNK
N_SAMPLE_BLK = DEC_BATCH // SEQ_BLK
N_STEPS = N_CHUNK + N_SAMPLE_BLK
N_TOK = N_STEPS * ROW_BLK
N_STATE_SEQ = BATCH + DEC_BATCH

LANE = 128
EVEN_W = 3200
ODD_W = 2560
E_MQ, E_MK, E_MV, E_MO, E_GQ, E_GK, E_GV, E_GR, E_SM = 0, 256, 512, 1024, 1536, 1792, 2048, 2560, 3072
O_XB, O_GB, O_TQ, O_TK, O_TV, O_TG = 0, 512, 1024, 1280, 1536, 2048
GLA_LEVELS = (32, 16, 8, 4, 2, 1)
ROUTE_E0 = 16
PROJ_TILE = 640

PAIRS = ((0, 1), (0, 2), (0, 3), (1, 2), (1, 3), (2, 3))
N_CLASS = N_GROUPS * len(PAIRS)
WIN = 16
CAP_WIN = ROW_BLK // WIN + N_CLASS
CAP_ROWS = CAP_WIN * WIN
N_WIN = N_STEPS * CAP_WIN
WPT = 16
TILE_ROWS = WPT * WIN
N_TILES = -(-N_WIN // WPT) + N_CLASS
VMEM_LIMIT = 58 * 1024 * 1024


def _logsig(x):
    return jnp.minimum(x, 0.0) - jnp.log1p(jnp.exp(-jnp.abs(x)))


def _softplus(x):
    return jnp.maximum(x, 0.0) + jnp.log1p(jnp.exp(-jnp.abs(x)))


def _sigmoid(x):
    return 1.0 / (1.0 + jnp.exp(-x))


def _silu(x):
    return x * _sigmoid(x)


def _gelu_tanh(x):
    return 0.5 * x * (1.0 + jnp.tanh(math.sqrt(2.0 / math.pi) * (x + 0.044715 * (x * x * x))))


def _dot(a, b):
    return jnp.dot(a, b, preferred_element_type=F32)


def _dot_nt(a, b):
    return lax.dot_general(a, b, (((1,), (1,)), ((), ())), preferred_element_type=F32)


def _dot_tn(a, b):
    return lax.dot_general(a, b, (((0,), (0,)), ((), ())), preferred_element_type=F32)


def _split2(x):
    hi = x.astype(BF16)
    return hi, (x - hi.astype(F32)).astype(BF16)


def _split3(x):
    hi = x.astype(BF16)
    r1 = x - hi.astype(F32)
    mid = r1.astype(BF16)
    lo = (r1 - mid.astype(F32)).astype(BF16)
    return hi, mid, lo


def _bdot(a, b):
    return lax.dot_general(a, b, (((2,), (1,)), ((0,), (0,))), preferred_element_type=F32)


def _bdot_nt(a, b):
    return lax.dot_general(a, b, (((2,), (2,)), ((0,), (0,))), preferred_element_type=F32)


def _bdot_tn(a, b):
    return lax.dot_general(a, b, (((1,), (1,)), ((0,), (0,))), preferred_element_type=F32)


def _stack3(x):
    return jnp.concatenate(_split3(x), axis=1)


def _rows_bdot(mat01x3, x3):
    m = jnp.broadcast_to(mat01x3[None], (x3.shape[0],) + mat01x3.shape)
    return _bdot(m, x3)


def _heads_nt(q, k, head_diag):
    kbd = jnp.where(head_diag, jnp.concatenate([k] * N_HEADS, axis=1), jnp.zeros((), k.dtype))
    return _bdot_nt(q, kbd)


def _head_norm(h, g):
    mu = jnp.mean(h, axis=-1, keepdims=True)
    d = h - mu
    var = jnp.mean(d * d, axis=-1, keepdims=True)
    return d * lax.rsqrt(var + HEAD_EPS) * g


def _layer_norm(x, g, b):
    mu = jnp.mean(x, axis=-1, keepdims=True)
    d = x - mu
    var = jnp.mean(d * d, axis=-1, keepdims=True)
    return d * lax.rsqrt(var + LN_EPS) * g + b


def _iota(shape, dim):
    return lax.broadcasted_iota(jnp.int32, shape, dim)


def _combine_ln(x1_ref, ys_ref, dsl_ref, ln_ref):
    dest = dsl_ref[:, 0:1]
    pos = _iota((ROW_BLK, CAP_ROWS), 1).astype(F32)
    sel = jnp.where(pos == dest, 1.0, 0.0).astype(BF16)
    moe = _dot(sel, ys_ref[...])
    return _layer_norm(DN_ALPHA * x1_ref[...] + moe, ln_ref[0:1, :], ln_ref[1:2, :])


def _load_x(first, head_refs, x_scr):
    j = pl.program_id(0)
    if first:
        xp_ref, xsm_ref = head_refs

        @pl.when(j < N_CHUNK)
        def _():
            x_scr[...] = xp_ref[...].reshape(ROW_BLK, D_MODEL)

        @pl.when(j >= N_CHUNK)
        def _():
            x_scr[...] = xsm_ref[...].reshape(ROW_BLK, D_MODEL)
    else:
        x_scr[...] = _combine_ln(*head_refs)


def _project_in(x_scr, win_ref, z_scr, width):
    xb = x_scr[...].astype(BF16)
    for c0 in range(0, width, PROJ_TILE):
        z_scr[:, :, c0:c0 + PROJ_TILE] = _dot(xb, win_ref[:, c0:c0 + PROJ_TILE]).reshape(SEQ_BLK, CHUNK, PROJ_TILE)


def _route(x1, xh, wr_ref, br_ref):
    xl = (x1 - xh.astype(F32)).astype(BF16)
    wh = wr_ref[:, 0:LANE]
    wl = wr_ref[:, LANE:2 * LANE]
    logits = _dot(xh, wh) + _dot(xh, wl) + _dot(xl, wh) + br_ref[...]
    lane = _iota(logits.shape, 1).astype(F32)
    neg = -jnp.inf
    big = float(4 * LANE)
    gl = jnp.where(lane < N_GROUPS, logits, neg)
    gmax = jnp.max(gl, axis=1, keepdims=True)
    gsum = jnp.sum(jnp.exp(gl - gmax), axis=1, keepdims=True)
    g_p = 1.0 / gsum
    g_idx = jnp.min(jnp.where(gl == gmax, lane, big), axis=1, keepdims=True)
    lo = ROUTE_E0 + EXPERTS_PER_GROUP * g_idx
    in_grp = (lane >= lo) & (lane < lo + EXPERTS_PER_GROUP)
    el = jnp.where(in_grp, logits, neg)
    e1 = jnp.max(el, axis=1, keepdims=True)
    i1 = jnp.min(jnp.where(el == e1, lane, big), axis=1, keepdims=True)
    el2 = jnp.where(lane == i1, neg, el)
    e2 = jnp.max(el2, axis=1, keepdims=True)
    i2 = jnp.min(jnp.where(el2 == e2, lane, big), axis=1, keepdims=True)
    t = jnp.exp(e2 - e1)
    w1 = g_p / (1.0 + t)
    w2 = g_p * t / (1.0 + t)
    first_lower = i1 < i2
    la = jnp.minimum(i1, i2) - lo
    lb = jnp.maximum(i1, i2) - lo
    pair = la * (7.0 - la) * 0.5 + (lb - la - 1.0)
    cls = g_idx * float(len(PAIRS)) + pair
    return cls, jnp.where(first_lower, w1, w2), jnp.where(first_lower, w2, w1)


def _mixer_tail(x_scr, y_scr, wout_ref, ln_ref, wr_ref, br_ref, lt_ref, ut_ref,
                x1_ref, xs_ref, gs_ref, dsl_ref, meta_ref):
    y = _dot(y_scr[...].reshape(ROW_BLK, D_MODEL).astype(BF16), wout_ref[...])
    x1 = _layer_norm(DN_ALPHA * x_scr[...] + y, ln_ref[0:1, :], ln_ref[1:2, :])
    x1_ref[...] = x1
    x1b = x1.astype(BF16)
    cls, g_a, g_b = _route(x1, x1b, wr_ref, br_ref)

    lane = _iota((ROW_BLK, LANE), 1).astype(F32)
    onehot = jnp.where(lane == cls, 1.0, 0.0)
    earlier = _dot(lt_ref[...], onehot.astype(BF16))
    rank = jnp.sum(earlier * onehot, axis=1, keepdims=True)
    cnt = jnp.sum(onehot, axis=0, keepdims=True)
    nwin = jnp.floor((cnt + (WIN - 1.0)) * (1.0 / WIN))
    woff = _dot(jnp.broadcast_to(nwin, (8, LANE)).astype(BF16), ut_ref[...])[0:1]
    dest = WIN * jnp.sum(onehot * woff, axis=1, keepdims=True) + rank
    pos = _iota((ROW_BLK, CAP_ROWS), 1).astype(F32)
    sel = jnp.where(pos == dest, 1.0, 0.0).astype(BF16)
    for c0 in range(0, D_MODEL, 256):
        xs_ref[:, c0:c0 + 256] = _dot_tn(sel, x1b[:, c0:c0 + 256]).astype(BF16)
    ga_hi, ga_lo = _split2(g_a)
    gb_hi, gb_lo = _split2(g_b)
    gpack = jnp.where(lane == 0, ga_hi.astype(F32), jnp.where(lane == 1, ga_lo.astype(F32),
            jnp.where(lane == 2, gb_hi.astype(F32), jnp.where(lane == 3, gb_lo.astype(F32), 0.0))))
    gs_ref[...] = _dot_tn(sel, gpack.astype(BF16))
    dsl_ref[...] = jnp.where(lane == 0, dest, 0.0)
    meta_ref[...] = jnp.concatenate([cnt, nwin, woff, jnp.zeros((5, LANE), F32)], axis=0)


def _init_states(j, pairs):
    @pl.when(j == 0)
    def _():
        for _, out_ref in pairs:
            out_ref[...] = jnp.zeros(out_ref.shape, out_ref.dtype)

    @pl.when(j >= N_CHUNK)
    def _():
        for in_ref, out_ref in pairs:
            out_ref[...] = in_ref[...]


def _even_kernel(*refs, first):
    nh = 2 if first else 4
    head_refs = refs[:nh]
    (win_ref, wout_ref, ln_ref, wr_ref, br_ref, lt_ref, ut_ref, cm_ref, bsm_ref, gm_ref, wa2_ref, ba_ref, gg_ref,
     c_in, n_in, m_in, s_in,
     x1_ref, xs_ref, gs_ref, dsl_ref, meta_ref, c_out, n_out, m_out, s_out,
     x_scr, z_scr, y_scr) = refs[nh:]
    j = pl.program_id(0)
    _init_states(j, ((c_in, c_out), (n_in, n_out), (m_in, m_out), (s_in, s_out)))
    _load_x(first, head_refs, x_scr)
    _project_in(x_scr, win_ref, z_scr, EVEN_W)

    rr = _iota((1, CHUNK, CHUNK), 1)
    cc = _iota((1, CHUNK, CHUNK), 2)
    causal = cc <= rr
    trow = _iota((1, CHUNK, 1), 1)
    tri = cm_ref[0:CHUNK, :]
    neg = -jnp.inf

    n_all = n_out[...]
    m_all = m_out[...]
    lane_m = _iota((1, 1, LANE), 2)
    n_new = []
    m_new = m_all
    slab = z_scr[:, :, E_SM:E_SM + LANE]
    pre = slab + bsm_ref[...]
    fcum = _rows_bdot(tri, _stack3(_logsig(pre)))
    at = jnp.swapaxes(jnp.concatenate([pre, fcum], axis=1), 1, 2)

    for h in range(N_HEADS):
        bc = fcum[:, :, N_HEADS + h:N_HEADS + h + 1]
        icol = pre[:, :, h:h + 1]
        br = at[:, N_HEADS + h:N_HEADS + h + 1, CHUNK:2 * CHUNK]
        ir = at[:, h:h + 1, 0:CHUNK]
        mprev = m_all[:, :, h:h + 1]
        logd = jnp.where(causal, bc - br + ir, neg)
        linter = bc + mprev
        mrow = jnp.maximum(linter, jnp.max(logd, axis=2, keepdims=True))
        wintra = jnp.exp(logd - mrow)
        winter = jnp.exp(linter - mrow)
        qh = z_scr[:, :, E_MQ + h * DK:E_MQ + (h + 1) * DK]
        kh = z_scr[:, :, E_MK + h * DK:E_MK + (h + 1) * DK] * (DK ** -0.5)
        vb = z_scr[:, :, E_MV + h * DV:E_MV + (h + 1) * DV].astype(BF16)
        qb = qh.astype(BF16)
        sm = _bdot_nt(qb, kh.astype(BF16)) * wintra
        cst = c_out[:, h]
        nrow = n_all[:, :, h * DK:(h + 1) * DK]
        num = _bdot(sm.astype(BF16), vb) + winter * _bdot(qb, cst.astype(BF16))
        den = jnp.sum(sm, axis=2, keepdims=True) + winter * jnp.sum(qh * nrow, axis=2, keepdims=True)
        hh = num / jnp.maximum(jnp.abs(den), jnp.exp(-mrow))
        blast = bc[:, CHUNK - 1:CHUNK, :]
        logw = blast - bc + icol
        mnew = jnp.maximum(blast + mprev, jnp.max(logw, axis=1, keepdims=True))
        kw = kh * jnp.exp(logw - mnew)
        decay = jnp.exp(blast + mprev - mnew)
        c_out[:, h] = decay * cst + _bdot_tn(kw.astype(BF16), vb)
        n_new.append(decay * nrow + jnp.sum(kw, axis=1, keepdims=True))
        m_new = jnp.where(lane_m == h, mnew, m_new)
        mo = z_scr[:, :, E_MO + h * DV:E_MO + (h + 1) * DV]
        y_scr[:, :, h * DV:(h + 1) * DV] = _sigmoid(mo) * _head_norm(hh, gm_ref[:, h * DV:(h + 1) * DV])

    n_out[...] = jnp.concatenate(n_new, axis=2)
    m_out[...] = m_new

    la = _logsig(_dot(slab.reshape(ROW_BLK, LANE).astype(BF16), wa2_ref[...]) + ba_ref[...]) * (1.0 / G_TAU)
    la = la.reshape(SEQ_BLK, CHUNK, N_HEADS * DK)
    la3 = _stack3(la)
    bg = _rows_bdot(tri, la3)
    gq = z_scr[:, :, E_GQ:E_GQ + N_HEADS * DK] * (DK ** -0.5)
    gk = z_scr[:, :, E_GK:E_GK + N_HEADS * DK]
    blast = bg[:, CHUNK - 1:CHUNK, :]
    qdec = (gq * jnp.exp(bg)).astype(BF16)
    kdec = (gk * jnp.exp(blast - bg)).astype(BF16)
    lat = jnp.swapaxes(jnp.concatenate([la, jnp.zeros_like(la)], axis=1), 1, 2)
    sdec = jnp.exp(jnp.sum(lat, axis=2, keepdims=True))
    gqb = gq.astype(BF16)
    gkb = gk.astype(BF16)
    hd_r = _iota((1, N_HEADS * CHUNK, N_HEADS * DK), 1) >> int(math.log2(CHUNK))
    hd_c = _iota((1, N_HEADS * CHUNK, N_HEADS * DK), 2) >> int(math.log2(DK))
    head_diag = hd_r == hd_c
    rr4 = _iota((1, CHUNK, N_HEADS * CHUNK), 1)
    cc4 = _iota((1, CHUNK, N_HEADS * CHUNK), 2) & (CHUNK - 1)
    att = jnp.where(cc4 == rr4, _heads_nt(gqb, gkb, head_diag), 0.0)
    for li, m in enumerate(GLA_LEVELS):
        r0 = CHUNK * (1 + li)
        ex = _rows_bdot(cm_ref[r0:r0 + CHUNK, :], la3)
        up = (trow & (2 * m - 1)) >= m
        dec = jnp.exp(ex)
        qt = jnp.where(up, gq * dec, 0.0).astype(BF16)
        kt = jnp.where(up, 0.0, gk * dec).astype(BF16)
        sh = int(math.log2(2 * m))
        att = att + jnp.where((rr4 >> sh) == (cc4 >> sh), _heads_nt(qt, kt, head_diag), 0.0)
    att = att.astype(BF16)
    for h in range(N_HEADS):
        hs = slice(h * DK, (h + 1) * DK)
        vb = z_scr[:, :, E_GV + h * DV:E_GV + (h + 1) * DV].astype(BF16)
        sst = s_out[:, h]
        og = _bdot(att[:, :, h * CHUNK:(h + 1) * CHUNK], vb) + _bdot(qdec[:, :, hs], sst.astype(BF16))
        s_out[:, h] = sdec[:, h * DK:(h + 1) * DK, :] * sst + _bdot_tn(kdec[:, :, hs], vb)
        gr = z_scr[:, :, E_GR + h * DV:E_GR + (h + 1) * DV]
        y_scr[:, :, HALF + h * DV:HALF + (h + 1) * DV] = _silu(gr) * _head_norm(og, gg_ref[:, h * DV:(h + 1) * DV])

    _mixer_tail(x_scr, y_scr, wout_ref, ln_ref, wr_ref, br_ref, lt_ref, ut_ref,
                x1_ref, xs_ref, gs_ref, dsl_ref, meta_ref)


def _odd_kernel(*refs, first):
    nh = 2 if first else 4
    head_refs = refs[:nh]
    (win_ref, wout_ref, ln_ref, wr_ref, br_ref, lt_ref, ut_ref, rope_ref, cw_ref, wri_ref, bri_ref, lam_ref, gt_ref,
     cv_in, h_in, r_in,
     x1_ref, xs_ref, gs_ref, dsl_ref, meta_ref, cv_out, h_out, r_out,
     x_scr, z_scr, y_scr, pad_scr) = refs[nh:]
    j = pl.program_id(0)
    _init_states(j, ((cv_in, cv_out), (h_in, h_out), (r_in, r_out)))
    _load_x(first, head_refs, x_scr)
    _project_in(x_scr, win_ref, z_scr, ODD_W)

    rr = _iota((1, CHUNK, CHUNK), 1)
    cc = _iota((1, CHUNK, CHUNK), 2)
    causal = cc <= rr
    dist = (rr - cc).astype(F32)
    tf = _iota((1, CHUNK, 1), 1).astype(F32)
    trow = _iota((ROW_BLK, 1), 0) & (CHUNK - 1)
    lane4 = _iota((ROW_BLK, N_HEADS * DK), 1)
    first_half = (lane4 & (DK - 1)) < (DK // 2)
    lgs = [math.log1p(-(2.0 ** (-5.0 - h))) for h in range(N_HEADS)]
    sp = _softplus(-lam_ref[...])

    def flat(a3):
        return a3.reshape(ROW_BLK, a3.shape[-1])

    def streams(a2):
        return a2.reshape(SEQ_BLK, CHUNK, a2.shape[-1])

    cosf = flat(jnp.broadcast_to(rope_ref[:, 0:N_HEADS * DK][None], (SEQ_BLK, CHUNK, N_HEADS * DK)))
    sinf = flat(jnp.broadcast_to(rope_ref[:, N_HEADS * DK:2 * N_HEADS * DK][None], (SEQ_BLK, CHUNK, N_HEADS * DK)))

    def rot(a):
        swapped = jnp.where(first_half, pltpu.roll(a, N_HEADS * DK - DK // 2, 1), pltpu.roll(a, DK // 2, 1))
        return a * cosf + swapped * sinf

    xb = z_scr[:, :, O_XB:O_XB + R_WIDTH]
    pad_scr[:, 8 - (CONV_W - 1):8, :] = cv_out[...]
    pad_scr[:, 8:8 + CHUNK, :] = xb
    xc = cw_ref[CONV_W:CONV_W + 1, :] + xb * cw_ref[CONV_W - 1:CONV_W, :]
    for d in range(1, CONV_W):
        xc = xc + pad_scr[:, 8 - d:8 - d + CHUNK, :] * cw_ref[CONV_W - 1 - d:CONV_W - d, :]
    cv_out[...] = xb[:, CHUNK - (CONV_W - 1):CHUNK, :]
    xc = flat(xc)
    xcb = xc.astype(BF16)
    halves = []
    for t in range(2):
        halves.append(_dot(xcb[:, t * 256:(t + 1) * 256], wri_ref[t]))
    r_pre = jnp.concatenate([halves[0][:, 0:256], halves[1][:, 0:256]], axis=1) + bri_ref[0:1, :]
    i_pre = jnp.concatenate([halves[0][:, 256:512], halves[1][:, 256:512]], axis=1) + bri_ref[1:2, :]
    log_a = (-LRU_C) * _sigmoid(r_pre) * sp
    a = jnp.exp(log_a)
    th = jnp.tanh(log_a)
    u = jnp.sqrt((-2.0 * th) / (1.0 - th)) * (_sigmoid(i_pre) * xc)
    h0 = flat(jnp.broadcast_to(h_out[...], (SEQ_BLK, CHUNK, R_WIDTH)))
    u = u + jnp.where(trow == 0, a * h0, 0.0)
    for sh in (1, 2, 4, 8, 16, 32):
        valid = trow >= sh
        u_prev = pltpu.roll(u, sh, 0)
        a_prev = pltpu.roll(a, sh, 0)
        u = jnp.where(valid, a * u_prev + u, u)
        a = jnp.where(valid, a * a_prev, a)
    h_out[...] = streams(u)[:, CHUNK - 1:CHUNK, :]
    y_scr[:, :, 0:R_WIDTH] = streams(_gelu_tanh(flat(z_scr[:, :, O_GB:O_GB + R_WIDTH])) * u)

    q = streams(rot(flat(z_scr[:, :, O_TQ:O_TQ + N_HEADS * DK])).astype(BF16))
    kf = streams(rot(flat(z_scr[:, :, O_TK:O_TK + N_HEADS * DK])) * (DK ** -0.5))
    for h in range(N_HEADS):
        hs = slice(h * DK, (h + 1) * DK)
        lg = lgs[h]
        dec = jnp.where(causal, jnp.exp(dist * lg), 0.0)
        w_inter = jnp.exp((tf + 1.0) * lg)
        w_key = jnp.exp((CHUNK - 1.0 - tf) * lg)
        s_decay = math.exp(CHUNK * lg)
        qh = q[:, :, hs]
        kh = kf[:, :, hs]
        vb = z_scr[:, :, O_TV + h * DV:O_TV + (h + 1) * DV].astype(BF16)
        sst = r_out[:, h]
        att = _bdot_nt(qh, kh.astype(BF16)) * dec
        o = _bdot(att.astype(BF16), vb) + w_inter * _bdot(qh, sst.astype(BF16))
        r_out[:, h] = s_decay * sst + _bdot_tn((kh * w_key).astype(BF16), vb)
        tg = z_scr[:, :, O_TG + h * DV:O_TG + (h + 1) * DV]
        y_scr[:, :, HALF + h * DV:HALF + (h + 1) * DV] = _silu(tg) * _head_norm(o, gt_ref[:, h * DV:(h + 1) * DV])

    _mixer_tail(x_scr, y_scr, wout_ref, ln_ref, wr_ref, br_ref, lt_ref, ut_ref,
                x1_ref, xs_ref, gs_ref, dsl_ref, meta_ref)


def _moe_kernel(nused_ref, ea_ref, eb_ref, wid_ref, xs_hbm, gs_hbm, wga_ref, wua_ref, wda_ref, wgb_ref, wub_ref,
                wdb_ref, ys_hbm, xbuf, gbuf, ybuf, sem_x, sem_g, sem_y):
    i = pl.program_id(0)
    n_used = nused_ref[0]
    slot = i % 2

    def x_copy(w, sl, k):
        return pltpu.make_async_copy(xs_hbm.at[w], xbuf.at[sl, k], sem_x.at[sl, k])

    def g_copy(w, sl, k):
        return pltpu.make_async_copy(gs_hbm.at[w], gbuf.at[sl, k], sem_g.at[sl, k])

    def y_copy(w, sl, k):
        return pltpu.make_async_copy(ybuf.at[sl, k], ys_hbm.at[w], sem_y.at[sl, k])

    def for_windows(tile, fn):
        for k in range(WPT):
            w = wid_ref[tile * WPT + k]

            @pl.when(w >= 0)
            def _():
                fn(w, k)

    def start_gather(tile, sl):
        def fn(w, k):
            x_copy(w, sl, k).start()
            g_copy(w, sl, k).start()
        for_windows(tile, fn)

    @pl.when(i == 0)
    def _():
        xbuf[...] = jnp.zeros(xbuf.shape, xbuf.dtype)
        gbuf[...] = jnp.zeros(gbuf.shape, gbuf.dtype)
        start_gather(0, 0)

    @pl.when(i + 1 < n_used)
    def _():
        start_gather(i + 1, 1 - slot)

    @pl.when(i < n_used)
    def _():
        def wait_in(w, k):
            x_copy(w, slot, k).wait()
            g_copy(w, slot, k).wait()
        for_windows(i, wait_in)

        @pl.when(i >= 2)
        def _():
            for_windows(i - 2, lambda w, k: y_copy(w, slot, k).wait())

        x = xbuf[slot].reshape(TILE_ROWS, D_MODEL)
        g = gbuf[slot].reshape(TILE_ROWS, LANE)
        g_a = g[:, 0:1] + g[:, 1:2]
        g_b = g[:, 2:3] + g[:, 3:4]
        h_a = _silu(_dot(x, wga_ref[0])) * _dot(x, wua_ref[0]) * g_a
        h_b = _silu(_dot(x, wgb_ref[0])) * _dot(x, wub_ref[0]) * g_b
        y = _dot(h_a.astype(BF16), wda_ref[0]) + _dot(h_b.astype(BF16), wdb_ref[0])
        ybuf[slot] = y.astype(BF16).reshape(WPT, WIN, D_MODEL)
        for_windows(i, lambda w, k: y_copy(w, slot, k).start())

        @pl.when(i == n_used - 1)
        def _():
            for_windows(i, lambda w, k: y_copy(w, slot, k).wait())

            @pl.when(i >= 1)
            def _():
                for_windows(i - 1, lambda w, k: y_copy(w, 1 - slot, k).wait())


def _final_kernel(x1_ref, ys_ref, dsl_ref, ln_ref, yp_ref, ysm_ref):
    j = pl.program_id(0)
    x2 = _combine_ln(x1_ref, ys_ref, dsl_ref, ln_ref).reshape(SEQ_BLK, CHUNK, D_MODEL)

    @pl.when(j < N_CHUNK)
    def _():
        yp_ref[...] = x2

    @pl.when(j >= N_CHUNK)
    def _():
        ysm_ref[...] = x2


def _const_spec(shape):
    nd = len(shape)
    return pl.BlockSpec(shape, lambda j: (0,) * nd, pipeline_mode=pl.Buffered(1))


def _row_spec(rows, width):
    return pl.BlockSpec((rows, width), lambda j: (j, 0))


def _prompt_spec():
    return pl.BlockSpec((SEQ_BLK, CHUNK, D_MODEL), lambda j: (0, jnp.minimum(j, N_CHUNK - 1), 0))


def _sample_spec():
    return pl.BlockSpec((SEQ_BLK, CHUNK, D_MODEL), lambda j: (jnp.maximum(j - N_CHUNK, 0), 0, 0))


def _state_in_spec(shape):
    nd = len(shape)
    return pl.BlockSpec((SEQ_BLK,) + shape[1:], lambda j: (jnp.maximum(j - N_CHUNK, 0),) + (0,) * (nd - 1))


def _state_out_spec(shape):
    nd = len(shape)
    return pl.BlockSpec((SEQ_BLK,) + shape[1:], lambda j: (jnp.maximum(j - N_CHUNK + 1, 0),) + (0,) * (nd - 1))


def _head_specs(first):
    if first:
        return [_prompt_spec(), _sample_spec()]
    return [_row_spec(ROW_BLK, D_MODEL), _row_spec(CAP_ROWS, D_MODEL), _row_spec(ROW_BLK, LANE),
            _const_spec((2, D_MODEL))]


def _mixer_call(kernel_fn, name, first, head, consts, states, width, extra_scratch=()):
    const_specs = [c[1] if isinstance(c, tuple) else _const_spec(c.shape) for c in consts]
    consts = [c[0] if isinstance(c, tuple) else c for c in consts]
    state_specs = [_state_in_spec(s.shape) for s in states]
    out_shapes = [jax.ShapeDtypeStruct((N_TOK, D_MODEL), F32),
                  jax.ShapeDtypeStruct((N_STEPS * CAP_ROWS, D_MODEL), BF16),
                  jax.ShapeDtypeStruct((N_STEPS * CAP_ROWS, LANE), F32),
                  jax.ShapeDtypeStruct((N_TOK, LANE), F32),
                  jax.ShapeDtypeStruct((N_STEPS * 8, LANE), F32)]
    out_specs = [_row_spec(ROW_BLK, D_MODEL), _row_spec(CAP_ROWS, D_MODEL), _row_spec(CAP_ROWS, LANE),
                 _row_spec(ROW_BLK, LANE), _row_spec(8, LANE)]
    for s in states:
        shp = (N_STATE_SEQ,) + s.shape[1:]
        out_shapes.append(jax.ShapeDtypeStruct(shp, F32))
        out_specs.append(_state_out_spec(shp))
    return pl.pallas_call(
        functools.partial(kernel_fn, first=first),
        grid=(N_STEPS,),
        in_specs=_head_specs(first) + const_specs + state_specs,
        out_specs=out_specs,
        out_shape=out_shapes,
        scratch_shapes=[pltpu.VMEM((ROW_BLK, D_MODEL), F32), pltpu.VMEM((SEQ_BLK, CHUNK, width), F32),
                        pltpu.VMEM((SEQ_BLK, CHUNK, D_MODEL), F32)] + list(extra_scratch),
        compiler_params=pltpu.CompilerParams(dimension_semantics=("arbitrary",), vmem_limit_bytes=VMEM_LIMIT),
        name=name,
    )(*head, *consts, *states)


def _moe_call(n_used, tile_ea, tile_eb, wid, xs, gs, wg, wu, wd, layer):
    xs_w = xs.reshape(N_WIN, WIN, D_MODEL)
    gs_w = gs.reshape(N_WIN, WIN, LANE)
    wspec = lambda shape, sel: pl.BlockSpec((None, 1) + shape,
                                            lambda i, nu, ea, eb, wi: (layer, (ea, eb)[sel][i], 0, 0))
    grid_spec = pltpu.PrefetchScalarGridSpec(
        num_scalar_prefetch=4,
        grid=(N_TILES,),
        in_specs=[pl.BlockSpec(memory_space=pl.ANY), pl.BlockSpec(memory_space=pl.ANY),
                  wspec((D_MODEL, D_EXPERT), 0), wspec((D_MODEL, D_EXPERT), 0), wspec((D_EXPERT, D_MODEL), 0),
                  wspec((D_MODEL, D_EXPERT), 1), wspec((D_MODEL, D_EXPERT), 1), wspec((D_EXPERT, D_MODEL), 1)],
        out_specs=pl.BlockSpec(memory_space=pl.ANY),
        scratch_shapes=[pltpu.VMEM((2, WPT, WIN, D_MODEL), BF16), pltpu.VMEM((2, WPT, WIN, LANE), F32),
                        pltpu.VMEM((2, WPT, WIN, D_MODEL), BF16),
                        pltpu.SemaphoreType.DMA((2, WPT)), pltpu.SemaphoreType.DMA((2, WPT)),
                        pltpu.SemaphoreType.DMA((2, WPT))])
    ys_w = pl.pallas_call(
        _moe_kernel,
        grid_spec=grid_spec,
        out_shape=jax.ShapeDtypeStruct((N_WIN, WIN, D_MODEL), BF16),
        input_output_aliases={4: 0},
        compiler_params=pltpu.CompilerParams(dimension_semantics=("arbitrary",), vmem_limit_bytes=VMEM_LIMIT),
        name="moe",
    )(n_used, tile_ea, tile_eb, wid, xs_w, gs_w, wg, wu, wd, wg, wu, wd)
    return ys_w.reshape(N_STEPS * CAP_ROWS, D_MODEL)


def _final_call(x1, ys, dsl, ln):
    return pl.pallas_call(
        _final_kernel,
        grid=(N_STEPS,),
        in_specs=[_row_spec(ROW_BLK, D_MODEL), _row_spec(CAP_ROWS, D_MODEL), _row_spec(ROW_BLK, LANE),
                  _const_spec((2, D_MODEL))],
        out_specs=[_prompt_spec(), _sample_spec()],
        out_shape=[jax.ShapeDtypeStruct((BATCH, SEQ, D_MODEL), F32),
                   jax.ShapeDtypeStruct((DEC_BATCH, DEC_SEQ, D_MODEL), F32)],
        compiler_params=pltpu.CompilerParams(dimension_semantics=("arbitrary",), vmem_limit_bytes=VMEM_LIMIT),
        name="final_ln",
    )(x1, ys, dsl, ln)


def _plan_kernel(nw_ref, wo_ref, nused_ref, ea_ref, eb_ref, wid_ref):
    n_pairs = len(PAIRS)

    def class_body(c, pos):
        tile0 = pos // WPT

        def block_body(b, pos):
            n = nw_ref[b * N_CLASS + c]
            base = b * CAP_WIN + wo_ref[b * N_CLASS + c] - pos

            def win_body(p, carry):
                wid_ref[p] = base + p
                return carry
            lax.fori_loop(pos, pos + n, win_body, 0)
            return pos + n
        pos = lax.fori_loop(0, N_STEPS, block_body, pos)
        padded = (pos + (WPT - 1)) // WPT * WPT

        def pad_body(p, carry):
            wid_ref[p] = -1
            return carry
        lax.fori_loop(pos, padded, pad_body, 0)
        grp = c // n_pairs
        p = c - grp * n_pairs
        ge3 = jnp.where(p >= 3, 1, 0)
        ge5 = jnp.where(p >= 5, 1, 0)
        e_a = grp * EXPERTS_PER_GROUP + ge3 + ge5
        e_b = grp * EXPERTS_PER_GROUP + p + 1 - 2 * ge3 - ge5

        def tile_body(t, carry):
            ea_ref[t] = e_a
            eb_ref[t] = e_b
            return carry
        lax.fori_loop(tile0, padded // WPT, tile_body, 0)
        return padded

    end = lax.fori_loop(0, N_CLASS, class_body, 0)
    n_used = end // WPT
    nused_ref[0] = n_used
    last = jnp.maximum(n_used - 1, 0)
    last_a = ea_ref[last]
    last_b = eb_ref[last]

    def tail_tile(t, carry):
        ea_ref[t] = last_a
        eb_ref[t] = last_b
        return carry
    lax.fori_loop(n_used, N_TILES, tail_tile, 0)

    def tail_wid(p, carry):
        wid_ref[p] = -1
        return carry
    lax.fori_loop(end, N_TILES * WPT, tail_wid, 0)


def _plan_call(meta):
    i32 = jnp.int32
    meta = meta.reshape(N_STEPS, 8, LANE)
    nw = meta[:, 1, :N_CLASS].astype(i32).reshape(-1)
    wo = meta[:, 2, :N_CLASS].astype(i32).reshape(-1)
    smem = pl.BlockSpec(memory_space=pltpu.SMEM)
    return pl.pallas_call(
        _plan_kernel,
        in_specs=[smem, smem],
        out_specs=[smem, smem, smem, smem],
        out_shape=[jax.ShapeDtypeStruct((1,), i32), jax.ShapeDtypeStruct((N_TILES,), i32),
                   jax.ShapeDtypeStruct((N_TILES,), i32), jax.ShapeDtypeStruct((N_TILES * WPT,), i32)],
        name="moe_plan",
    )(nw, wo)


def _gla_sum_matrix():
    t = np.arange(CHUNK)
    mats = [(t[None, :] <= t[:, None])]
    for m in GLA_LEVELS:
        mid = (t // (2 * m)) * (2 * m) + m
        upper = (t % (2 * m)) >= m
        up = upper[:, None] & (t[None, :] > mid[:, None]) & (t[None, :] <= t[:, None])
        lo = (~upper)[:, None] & (t[None, :] > t[:, None]) & (t[None, :] <= mid[:, None])
        mats.append(up | lo)
    mat = np.concatenate(mats, axis=0).astype(np.float32)
    return jnp.asarray(np.concatenate([mat, mat, mat], axis=1), dtype=BF16)


def _rope_table():
    inv = ROPE_BASE ** (-jnp.arange(0, DK, 2, dtype=F32) / DK)
    pos = jnp.concatenate([jnp.arange(SEQ, dtype=F32)] + [PAST_LEN + jnp.arange(DEC_SEQ, dtype=F32)] * N_SAMPLE_BLK)
    ang = pos[:, None] * inv[None]
    cos, sin = jnp.cos(ang), jnp.sin(ang)
    cosf = jnp.tile(jnp.concatenate([cos, cos], axis=1), (1, N_HEADS))
    sinf = jnp.tile(jnp.concatenate([-sin, sin], axis=1), (1, N_HEADS))
    return jnp.concatenate([cosf, sinf], axis=1)


def _pad_lanes(a, width=LANE):
    return jnp.pad(a, [(0, 0)] * (a.ndim - 1) + [(0, width - a.shape[-1])])


def _router_pack(w_grp, b_grp, w_rt, b_rt):
    w = jnp.zeros((D_MODEL, LANE), F32).at[:, 0:N_GROUPS].set(w_grp).at[:, ROUTE_E0:ROUTE_E0 + N_EXPERTS].set(w_rt)
    wh = w.astype(BF16)
    wl = (w - wh.astype(F32)).astype(BF16)
    b = jnp.zeros((1, LANE), F32).at[0, 0:N_GROUPS].set(b_grp).at[0, ROUTE_E0:ROUTE_E0 + N_EXPERTS].set(b_rt)
    return jnp.concatenate([wh, wl], axis=1), b


def kernel(x_prompt, x_sample, state_mlstm_c, state_mlstm_n, state_mlstm_m, state_gla, state_conv, state_rglru, state_ret, w_in_even, b_mlstm_i, b_mlstm_f, g_mlstm_norm, w_gla_a2, b_gla_a, g_gla_norm, w_out_even, w_in_odd, w_conv, b_conv, w_lru_r, b_lru_r, w_lru_i, b_lru_i, lru_lambda, g_ret_norm, w_out_odd, ln1_g, ln1_b, ln2_g, ln2_b, w_router_group, b_router_group, w_router_expert, b_router_expert, w_exp_gate, w_exp_up, w_exp_down):
    cm = _gla_sum_matrix()
    rope = _rope_table()
    tt = np.arange(ROW_BLK)
    lower_strict = jnp.asarray((tt[None, :] < tt[:, None]).astype(np.float32), dtype=BF16)
    ll = np.arange(LANE)
    upper_strict = jnp.asarray((ll[:, None] < ll[None, :]).astype(np.float32), dtype=BF16)

    wg_all, wu_all, wd_all = w_exp_gate.astype(BF16), w_exp_up.astype(BF16), w_exp_down.astype(BF16)
    head = [x_prompt.astype(F32), x_sample.astype(F32)]
    new = {k: [] for k in ("c", "n", "m", "g", "conv", "h", "r")}
    for layer in range(DEPTH):
        li = layer // 2
        first = layer == 0
        wr, br = _router_pack(w_router_group[layer], b_router_group[layer], w_router_expert[layer],
                              b_router_expert[layer])
        ln1 = jnp.stack([ln1_g[layer], ln1_b[layer]])
        common = [ln1, wr, br, lower_strict, upper_strict]
        if layer % 2 == 0:
            w = w_in_even[li]
            win = jnp.concatenate([w[:, 0:1024], w[:, 1032:1544], w[:, 1544:2568], w[:, 2584:3096], w[:, 1024:1032],
                                   w[:, 2568:2584], jnp.zeros((D_MODEL, EVEN_W - 3096), w.dtype)], axis=1).astype(BF16)
            bsm = jnp.zeros((1, LANE), F32).at[0, 0:N_HEADS].set(b_mlstm_i[li]).at[0, N_HEADS:2 * N_HEADS].set(b_mlstm_f[li])
            wa2 = jnp.zeros((LANE, N_HEADS * DK), F32).at[2 * N_HEADS:2 * N_HEADS + G_RANK].set(w_gla_a2[li]).astype(BF16)
            consts = [win, w_out_even[li].astype(BF16)] + common + [cm, bsm, g_mlstm_norm[li][None, :], wa2,
                                                                     b_gla_a[li][None, :], g_gla_norm[li][None, :]]
            states = [state_mlstm_c[li].astype(F32), state_mlstm_n[li].astype(F32).reshape(DEC_BATCH, 1, N_HEADS * DK),
                      _pad_lanes(state_mlstm_m[li].astype(F32))[:, None, :], state_gla[li].astype(F32)]
            x1, xs, gs, dsl, meta, c, n, m, g = _mixer_call(_even_kernel, "even_mixer", first, head, consts, states,
                                                            EVEN_W)
            new["c"].append(c)
            new["n"].append(n.reshape(N_STATE_SEQ, N_HEADS, DK))
            new["m"].append(m[:, 0, 0:N_HEADS])
            new["g"].append(g)
        else:
            cw = jnp.concatenate([w_conv[li], b_conv[li][None, :]], axis=0)
            wr4 = w_lru_r[li].reshape(2, 4, R_BS, R_BS)
            wi4 = w_lru_i[li].reshape(2, 4, R_BS, R_BS)
            eye4 = jnp.eye(4, dtype=F32)
            bd = lambda w4: jnp.einsum("tncd,nm->tncmd", w4, eye4).reshape(2, 256, 256)
            wri = jnp.concatenate([bd(wr4), bd(wi4)], axis=2).astype(BF16)
            bri = jnp.stack([b_lru_r[li], b_lru_i[li]])
            consts = [w_in_odd[li].astype(BF16), w_out_odd[li].astype(BF16)] + common + [
                (rope, pl.BlockSpec((CHUNK, 2 * N_HEADS * DK), lambda j: (j, 0))), cw, wri, bri,
                lru_lambda[li][None, :], g_ret_norm[li][None, :]]
            states = [state_conv[li].astype(F32), state_rglru[li].astype(F32)[:, None, :], state_ret[li].astype(F32)]
            x1, xs, gs, dsl, meta, cv, hh, r = _mixer_call(_odd_kernel, "odd_mixer", first, head, consts, states, ODD_W,
                                                           extra_scratch=[pltpu.VMEM((SEQ_BLK, 8 + CHUNK, R_WIDTH), F32)])
            new["conv"].append(cv)
            new["h"].append(hh[:, 0, :])
            new["r"].append(r)
        n_used, tile_ea, tile_eb, wid = _plan_call(meta)
        ys = _moe_call(n_used, tile_ea, tile_eb, wid, xs, gs, wg_all, wu_all, wd_all, layer)
        head = [x1, ys, dsl, jnp.stack([ln2_g[layer], ln2_b[layer]])]

    y_prompt, y_sample = _final_call(*head)
    st = {k: jnp.stack(v) for k, v in new.items()}
    order = ("c", "n", "m", "g", "conv", "h", "r")
    dtypes = (state_mlstm_c.dtype, state_mlstm_n.dtype, state_mlstm_m.dtype, state_gla.dtype, state_conv.dtype,
              state_rglru.dtype, state_ret.dtype)
    prompt_states = tuple(st[k][:, :BATCH].astype(d) for k, d in zip(order, dtypes))
    sample_states = tuple(st[k][:, BATCH:].astype(d) for k, d in zip(order, dtypes))
    return (y_prompt.astype(x_prompt.dtype), y_sample.astype(x_sample.dtype)) + prompt_states + sample_states
```

```python
import functools
import math

import numpy as np
import jax
import jax.numpy as jnp
from jax import lax
from jax.experimental import pallas as pl
from jax.experimental.pallas import tpu as pltpu

F32 = jnp.float32
BF16 = jnp.bfloat16

D_MODEL = 1024
BATCH = 8
SEQ = 2048
DEPTH = 4
DEC_BATCH = 16
DEC_SEQ = 64
PAST_LEN = 1024
CHUNK = 64
HALF = D_MODEL // 2
N_HEADS = 4
DK = 64
DV = 128
G_RANK = 16
G_TAU = 16.0
R_WIDTH = HALF
R_BS = 64
CONV_W = 4
LRU_C = 8.0
ROPE_BASE = 10000.0
N_GROUPS = 4
EXPERTS_PER_GROUP = 4
N_EXPERTS = 16
D_EXPERT = D_MODEL // 4
DN_ALPHA = (2 * DEPTH) ** 0.25
LN_EPS = 1e-5
HEAD_EPS = 1e-6

SEQ_BLK = 8
ROW_BLK = SEQ_BLK * CHUNK
N_CHUNK = SEQ // CHUNK
N_SAMPLE_BLK = DEC_BATCH // SEQ_BLK
N_STEPS = N_CHUNK + N_SAMPLE_BLK
N_TOK = N_STEPS * ROW_BLK
N_STATE_SEQ = BATCH + DEC_BATCH

LANE = 128
EVEN_W = 3200
ODD_W = 2560
E_MQ, E_MK, E_MV, E_MO, E_GQ, E_GK, E_GV, E_GR, E_SM = 0, 256, 512, 1024, 1536, 1792, 2048, 2560, 3072
O_XB, O_GB, O_TQ, O_TK, O_TV, O_TG = 0, 512, 1024, 1280, 1536, 2048
GLA_LEVELS = (32, 16, 8, 4, 2, 1)
ROUTE_E0 = 16
PROJ_TILE = 640

PAIRS = ((0, 1), (0, 2), (0, 3), (1, 2), (1, 3), (2, 3))
N_CLASS = N_GROUPS * len(PAIRS)
WIN = 16
CAP_WIN = ROW_BLK // WIN + N_CLASS
CAP_ROWS = CAP_WIN * WIN
N_WIN = N_STEPS * CAP_WIN
WPT = 16
TILE_ROWS = WPT * WIN
N_TILES = -(-N_WIN // WPT) + N_CLASS
VMEM_LIMIT = 58 * 1024 * 1024


def _logsig(x):
    return jnp.minimum(x, 0.0) - jnp.log1p(jnp.exp(-jnp.abs(x)))


def _softplus(x):
    return jnp.maximum(x, 0.0) + jnp.log1p(jnp.exp(-jnp.abs(x)))


def _sigmoid(x):
    return 1.0 / (1.0 + jnp.exp(-x))


def _silu(x):
    return x * _sigmoid(x)


def _gelu_tanh(x):
    return 0.5 * x * (1.0 + jnp.tanh(math.sqrt(2.0 / math.pi) * (x + 0.044715 * (x * x * x))))


def _dot(a, b):
    return jnp.dot(a, b, preferred_element_type=F32)


def _dot_nt(a, b):
    return lax.dot_general(a, b, (((1,), (1,)), ((), ())), preferred_element_type=F32)


def _dot_tn(a, b):
    return lax.dot_general(a, b, (((0,), (0,)), ((), ())), preferred_element_type=F32)


def _split2(x):
    hi = x.astype(BF16)
    return hi, (x - hi.astype(F32)).astype(BF16)


def _split3(x):
    hi = x.astype(BF16)
    r1 = x - hi.astype(F32)
    mid = r1.astype(BF16)
    lo = (r1 - mid.astype(F32)).astype(BF16)
    return hi, mid, lo


def _bdot(a, b):
    return lax.dot_general(a, b, (((2,), (1,)), ((0,), (0,))), preferred_element_type=F32)


def _bdot_nt(a, b):
    return lax.dot_general(a, b, (((2,), (2,)), ((0,), (0,))), preferred_element_type=F32)


def _bdot_tn(a, b):
    return lax.dot_general(a, b, (((1,), (1,)), ((0,), (0,))), preferred_element_type=F32)


def _stack3(x):
    return jnp.concatenate(_split3(x), axis=1)


def _rows_bdot(mat01x3, x3):
    m = jnp.broadcast_to(mat01x3[None], (x3.shape[0],) + mat01x3.shape)
    return _bdot(m, x3)


def _heads_nt(q, k, head_diag):
    kbd = jnp.where(head_diag, jnp.concatenate([k] * N_HEADS, axis=1), jnp.zeros((), k.dtype))
    return _bdot_nt(q, kbd)


def _head_norm(h, g):
    mu = jnp.mean(h, axis=-1, keepdims=True)
    d = h - mu
    var = jnp.mean(d * d, axis=-1, keepdims=True)
    return d * lax.rsqrt(var + HEAD_EPS) * g


def _layer_norm(x, g, b):
    mu = jnp.mean(x, axis=-1, keepdims=True)
    d = x - mu
    var = jnp.mean(d * d, axis=-1, keepdims=True)
    return d * lax.rsqrt(var + LN_EPS) * g + b


def _iota(shape, dim):
    return lax.broadcasted_iota(jnp.int32, shape, dim)


def _combine_ln(x1_ref, ys_ref, dsl_ref, ln_ref):
    dest = dsl_ref[:, 0:1]
    pos = _iota((ROW_BLK, CAP_ROWS), 1).astype(F32)
    sel = jnp.where(pos == dest, 1.0, 0.0).astype(BF16)
    moe = _dot(sel, ys_ref[...])
    return _layer_norm(DN_ALPHA * x1_ref[...] + moe, ln_ref[0:1, :], ln_ref[1:2, :])


def _load_x(first, head_refs, x_scr):
    j = pl.program_id(0)
    if first:
        xp_ref, xsm_ref = head_refs

        @pl.when(j < N_CHUNK)
        def _():
            x_scr[...] = xp_ref[...].reshape(ROW_BLK, D_MODEL)

        @pl.when(j >= N_CHUNK)
        def _():
            x_scr[...] = xsm_ref[...].reshape(ROW_BLK, D_MODEL)
    else:
        x_scr[...] = _combine_ln(*head_refs)


def _project_in(x_scr, win_ref, z_scr, width):
    xb = x_scr[...].astype(BF16)
    for c0 in range(0, width, PROJ_TILE):
        z_scr[:, :, c0:c0 + PROJ_TILE] = _dot(xb, win_ref[:, c0:c0 + PROJ_TILE]).reshape(SEQ_BLK, CHUNK, PROJ_TILE)


def _route(x1, xh, wr_ref, br_ref):
    xl = (x1 - xh.astype(F32)).astype(BF16)
    wh = wr_ref[:, 0:LANE]
    wl = wr_ref[:, LANE:2 * LANE]
    logits = _dot(xh, wh) + _dot(xh, wl) + _dot(xl, wh) + br_ref[...]
    lane = _iota(logits.shape, 1).astype(F32)
    neg = -jnp.inf
    big = float(4 * LANE)
    gl = jnp.where(lane < N_GROUPS, logits, neg)
    gmax = jnp.max(gl, axis=1, keepdims=True)
    gsum = jnp.sum(jnp.exp(gl - gmax), axis=1, keepdims=True)
    g_p = 1.0 / gsum
    g_idx = jnp.min(jnp.where(gl == gmax, lane, big), axis=1, keepdims=True)
    lo = ROUTE_E0 + EXPERTS_PER_GROUP * g_idx
    in_grp = (lane >= lo) & (lane < lo + EXPERTS_PER_GROUP)
    el = jnp.where(in_grp, logits, neg)
    e1 = jnp.max(el, axis=1, keepdims=True)
    i1 = jnp.min(jnp.where(el == e1, lane, big), axis=1, keepdims=True)
    el2 = jnp.where(lane == i1, neg, el)
    e2 = jnp.max(el2, axis=1, keepdims=True)
    i2 = jnp.min(jnp.where(el2 == e2, lane, big), axis=1, keepdims=True)
    t = jnp.exp(e2 - e1)
    w1 = g_p / (1.0 + t)
    w2 = g_p * t / (1.0 + t)
    first_lower = i1 < i2
    la = jnp.minimum(i1, i2) - lo
    lb = jnp.maximum(i1, i2) - lo
    pair = la * (7.0 - la) * 0.5 + (lb - la - 1.0)
    cls = g_idx * float(len(PAIRS)) + pair
    return cls, jnp.where(first_lower, w1, w2), jnp.where(first_lower, w2, w1)


def _mixer_tail(x_scr, y_scr, wout_ref, ln_ref, wr_ref, br_ref, lt_ref, ut_ref,
                x1_ref, xs_ref, gs_ref, dsl_ref, meta_ref):
    y = _dot(y_scr[...].reshape(ROW_BLK, D_MODEL).astype(BF16), wout_ref[...])
    x1 = _layer_norm(DN_ALPHA * x_scr[...] + y, ln_ref[0:1, :], ln_ref[1:2, :])
    x1_ref[...] = x1
    x1b = x1.astype(BF16)
    cls, g_a, g_b = _route(x1, x1b, wr_ref, br_ref)

    lane = _iota((ROW_BLK, LANE), 1).astype(F32)
    onehot = jnp.where(lane == cls, 1.0, 0.0)
    earlier = _dot(lt_ref[...], onehot.astype(BF16))
    rank = jnp.sum(earlier * onehot, axis=1, keepdims=True)
    cnt = jnp.sum(onehot, axis=0, keepdims=True)
    nwin = jnp.floor((cnt + (WIN - 1.0)) * (1.0 / WIN))
    woff = _dot(jnp.broadcast_to(nwin, (8, LANE)).astype(BF16), ut_ref[...])[0:1]
    dest = WIN * jnp.sum(onehot * woff, axis=1, keepdims=True) + rank
    dhi = jnp.floor(dest * (1.0 / WIN))
    dpack = jnp.where(lane == 0, dhi, jnp.where(lane == 1, dest - WIN * dhi, 0.0)).astype(BF16)
    pick = jnp.where(_iota((8, LANE), 0) == _iota((8, LANE), 1), 1.0, 0.0).astype(BF16)
    drow = _dot_nt(pick, dpack)
    dest_row = WIN * drow[0:1, :] + drow[1:2, :]
    pos = _iota((CAP_ROWS, ROW_BLK), 0).astype(F32)
    sel = jnp.where(pos == dest_row, 1.0, 0.0).astype(BF16)
    for c0 in range(0, D_MODEL, 256):
        xs_ref[:, c0:c0 + 256] = _dot(sel, x1b[:, c0:c0 + 256]).astype(BF16)
    ga_hi, ga_lo = _split2(g_a)
    gb_hi, gb_lo = _split2(g_b)
    gpack = jnp.where(lane == 0, ga_hi.astype(F32), jnp.where(lane == 1, ga_lo.astype(F32),
            jnp.where(lane == 2, gb_hi.astype(F32), jnp.where(lane == 3, gb_lo.astype(F32), 0.0))))
    gs_ref[...] = _dot(sel, gpack.astype(BF16))
    dsl_ref[...] = jnp.where(lane == 0, dest, 0.0)
    meta_ref[...] = jnp.concatenate([cnt, nwin, woff, jnp.zeros((5, LANE), F32)], axis=0)


def _init_states(j, pairs):
    @pl.when(j == 0)
    def _():
        for _, out_ref in pairs:
            out_ref[...] = jnp.zeros(out_ref.shape, out_ref.dtype)

    @pl.when(j >= N_CHUNK)
    def _():
        for in_ref, out_ref in pairs:
            out_ref[...] = in_ref[...]


def _even_kernel(*refs, first):
    nh = 2 if first else 4
    head_refs = refs[:nh]
    (win_ref, wout_ref, ln_ref, wr_ref, br_ref, lt_ref, ut_ref, cm_ref, bsm_ref, gm_ref, wa2_ref, ba_ref, gg_ref,
     c_in, n_in, m_in, s_in,
     x1_ref, xs_ref, gs_ref, dsl_ref, meta_ref, c_out, n_out, m_out, s_out,
     x_scr, z_scr, y_scr) = refs[nh:]
    j = pl.program_id(0)
    _init_states(j, ((c_in, c_out), (n_in, n_out), (m_in, m_out), (s_in, s_out)))
    _load_x(first, head_refs, x_scr)
    _project_in(x_scr, win_ref, z_scr, EVEN_W)

    rr = _iota((1, CHUNK, CHUNK), 1)
    cc = _iota((1, CHUNK, CHUNK), 2)
    causal = cc <= rr
    trow = _iota((1, CHUNK, 1), 1)
    tri = cm_ref[0:CHUNK, :]
    neg = -jnp.inf

    n_all = n_out[...]
    m_all = m_out[...]
    lane_m = _iota((1, 1, LANE), 2)
    n_new = []
    m_new = m_all
    slab = z_scr[:, :, E_SM:E_SM + LANE]
    pre = slab + bsm_ref[...]
    fcum = _rows_bdot(tri, _stack3(_logsig(pre)))
    at = jnp.swapaxes(jnp.concatenate([pre, fcum], axis=1), 1, 2)

    for h in range(N_HEADS):
        bc = fcum[:, :, N_HEADS + h:N_HEADS + h + 1]
        icol = pre[:, :, h:h + 1]
        br = at[:, N_HEADS + h:N_HEADS + h + 1, CHUNK:2 * CHUNK]
        ir = at[:, h:h + 1, 0:CHUNK]
        mprev = m_all[:, :, h:h + 1]
        logd = jnp.where(causal, bc - br + ir, neg)
        linter = bc + mprev
        mrow = jnp.maximum(linter, jnp.max(logd, axis=2, keepdims=True))
        wintra = jnp.exp(logd - mrow)
        winter = jnp.exp(linter - mrow)
        qh = z_scr[:, :, E_MQ + h * DK:E_MQ + (h + 1) * DK]
        kh = z_scr[:, :, E_MK + h * DK:E_MK + (h + 1) * DK] * (DK ** -0.5)
        vb = z_scr[:, :, E_MV + h * DV:E_MV + (h + 1) * DV].astype(BF16)
        qb = qh.astype(BF16)
        sm = _bdot_nt(qb, kh.astype(BF16)) * wintra
        cst = c_out[:, h]
        nrow = n_all[:, :, h * DK:(h + 1) * DK]
        num = _bdot(sm.astype(BF16), vb) + winter * _bdot(qb, cst.astype(BF16))
        den = jnp.sum(sm, axis=2, keepdims=True) + winter * jnp.sum(qh * nrow, axis=2, keepdims=True)
        hh = num / jnp.maximum(jnp.abs(den), jnp.exp(-mrow))
        blast = bc[:, CHUNK - 1:CHUNK, :]
        logw = blast - bc + icol
        mnew = jnp.maximum(blast + mprev, jnp.max(logw, axis=1, keepdims=True))
        kw = kh * jnp.exp(logw - mnew)
        decay = jnp.exp(blast + mprev - mnew)
        c_out[:, h] = decay * cst + _bdot_tn(kw.astype(BF16), vb)
        n_new.append(decay * nrow + jnp.sum(kw, axis=1, keepdims=True))
        m_new = jnp.where(lane_m == h, mnew, m_new)
        mo = z_scr[:, :, E_MO + h * DV:E_MO + (h + 1) * DV]
        y_scr[:, :, h * DV:(h + 1) * DV] = _sigmoid(mo) * _head_norm(hh, gm_ref[:, h * DV:(h + 1) * DV])

    n_out[...] = jnp.concatenate(n_new, axis=2)
    m_out[...] = m_new

    la = _logsig(_dot(slab.reshape(ROW_BLK, LANE).astype(BF16), wa2_ref[...]) + ba_ref[...]) * (1.0 / G_TAU)
    la = la.reshape(SEQ_BLK, CHUNK, N_HEADS * DK)
    la3 = _stack3(la)
    bg = _rows_bdot(tri, la3)
    gq = z_scr[:, :, E_GQ:E_GQ + N_HEADS * DK] * (DK ** -0.5)
    gk = z_scr[:, :, E_GK:E_GK + N_HEADS * DK]
    blast = bg[:, CHUNK - 1:CHUNK, :]
    qdec = (gq * jnp.exp(bg)).astype(BF16)
    kdec = (gk * jnp.exp(blast - bg)).astype(BF16)
    lat = jnp.swapaxes(jnp.concatenate([la, jnp.zeros_like(la)], axis=1), 1, 2)
    sdec = jnp.exp(jnp.sum(lat, axis=2, keepdims=True))
    gqb = gq.astype(BF16)
    gkb = gk.astype(BF16)
    hd_r = _iota((1, N_HEADS * CHUNK, N_HEADS * DK), 1) >> int(math.log2(CHUNK))
    hd_c = _iota((1, N_HEADS * CHUNK, N_HEADS * DK), 2) >> int(math.log2(DK))
    head_diag = hd_r == hd_c
    rr4 = _iota((1, CHUNK, N_HEADS * CHUNK), 1)
    cc4 = _iota((1, CHUNK, N_HEADS * CHUNK), 2) & (CHUNK - 1)
    att = jnp.where(cc4 == rr4, _heads_nt(gqb, gkb, head_diag), 0.0)
    for li, m in enumerate(GLA_LEVELS):
        r0 = CHUNK * (1 + li)
        ex = _rows_bdot(cm_ref[r0:r0 + CHUNK, :], la3)
        up = (trow & (2 * m - 1)) >= m
        dec = jnp.exp(ex)
        qt = jnp.where(up, gq * dec, 0.0).astype(BF16)
        kt = jnp.where(up, 0.0, gk * dec).astype(BF16)
        sh = int(math.log2(2 * m))
        att = att + jnp.where((rr4 >> sh) == (cc4 >> sh), _heads_nt(qt, kt, head_diag), 0.0)
    att = att.astype(BF16)
    for h in range(N_HEADS):
        hs = slice(h * DK, (h + 1) * DK)
        vb = z_scr[:, :, E_GV + h * DV:E_GV + (h + 1) * DV].astype(BF16)
        sst = s_out[:, h]
        og = _bdot(att[:, :, h * CHUNK:(h + 1) * CHUNK], vb) + _bdot(qdec[:, :, hs], sst.astype(BF16))
        s_out[:, h] = sdec[:, h * DK:(h + 1) * DK, :] * sst + _bdot_tn(kdec[:, :, hs], vb)
        gr = z_scr[:, :, E_GR + h * DV:E_GR + (h + 1) * DV]
        y_scr[:, :, HALF + h * DV:HALF + (h + 1) * DV] = _silu(gr) * _head_norm(og, gg_ref[:, h * DV:(h + 1) * DV])

    _mixer_tail(x_scr, y_scr, wout_ref, ln_ref, wr_ref, br_ref, lt_ref, ut_ref,
                x1_ref, xs_ref, gs_ref, dsl_ref, meta_ref)


def _odd_kernel(*refs, first):
    nh = 2 if first else 4
    head_refs = refs[:nh]
    (win_ref, wout_ref, ln_ref, wr_ref, br_ref, lt_ref, ut_ref, rope_ref, cw_ref, wri_ref, bri_ref, lam_ref, gt_ref,
     cv_in, h_in, r_in,
     x1_ref, xs_ref, gs_ref, dsl_ref, meta_ref, cv_out, h_out, r_out,
     x_scr, z_scr, y_scr, pad_scr) = refs[nh:]
    j = pl.program_id(0)
    _init_states(j, ((cv_in, cv_out), (h_in, h_out), (r_in, r_out)))
    _load_x(first, head_refs, x_scr)
    _project_in(x_scr, win_ref, z_scr, ODD_W)

    rr = _iota((1, CHUNK, CHUNK), 1)
    cc = _iota((1, CHUNK, CHUNK), 2)
    causal = cc <= rr
    dist = (rr - cc).astype(F32)
    tf = _iota((1, CHUNK, 1), 1).astype(F32)
    trow = _iota((ROW_BLK, 1), 0) & (CHUNK - 1)
    lane4 = _iota((ROW_BLK, N_HEADS * DK), 1)
    first_half = (lane4 & (DK - 1)) < (DK // 2)
    lgs = [math.log1p(-(2.0 ** (-5.0 - h))) for h in range(N_HEADS)]
    sp = _softplus(-lam_ref[...])

    def flat(a3):
        return a3.reshape(ROW_BLK, a3.shape[-1])

    def streams(a2):
        return a2.reshape(SEQ_BLK, CHUNK, a2.shape[-1])

    cosf = flat(jnp.broadcast_to(rope_ref[:, 0:N_HEADS * DK][None], (SEQ_BLK, CHUNK, N_HEADS * DK)))
    sinf = flat(jnp.broadcast_to(rope_ref[:, N_HEADS * DK:2 * N_HEADS * DK][None], (SEQ_BLK, CHUNK, N_HEADS * DK)))

    def rot(a):
        swapped = jnp.where(first_half, pltpu.roll(a, N_HEADS * DK - DK // 2, 1), pltpu.roll(a, DK // 2, 1))
        return a * cosf + swapped * sinf

    xb = z_scr[:, :, O_XB:O_XB + R_WIDTH]
    pad_scr[:, 8 - (CONV_W - 1):8, :] = cv_out[...]
    pad_scr[:, 8:8 + CHUNK, :] = xb
    xc = cw_ref[CONV_W:CONV_W + 1, :] + xb * cw_ref[CONV_W - 1:CONV_W, :]
    for d in range(1, CONV_W):
        xc = xc + pad_scr[:, 8 - d:8 - d + CHUNK, :] * cw_ref[CONV_W - 1 - d:CONV_W - d, :]
    cv_out[...] = xb[:, CHUNK - (CONV_W - 1):CHUNK, :]
    xc = flat(xc)
    xcb = xc.astype(BF16)
    halves = []
    for t in range(2):
        halves.append(_dot(xcb[:, t * 256:(t + 1) * 256], wri_ref[t]))
    r_pre = jnp.concatenate([halves[0][:, 0:256], halves[1][:, 0:256]], axis=1) + bri_ref[0:1, :]
    i_pre = jnp.concatenate([halves[0][:, 256:512], halves[1][:, 256:512]], axis=1) + bri_ref[1:2, :]
    log_a = (-LRU_C) * _sigmoid(r_pre) * sp
    a = jnp.exp(log_a)
    th = jnp.tanh(log_a)
    u = jnp.sqrt((-2.0 * th) / (1.0 - th)) * (_sigmoid(i_pre) * xc)
    h0 = flat(jnp.broadcast_to(h_out[...], (SEQ_BLK, CHUNK, R_WIDTH)))
    u = u + jnp.where(trow == 0, a * h0, 0.0)
    for sh in (1, 2, 4, 8, 16, 32):
        valid = trow >= sh
        u_prev = pltpu.roll(u, sh, 0)
        a_prev = pltpu.roll(a, sh, 0)
        u = jnp.where(valid, a * u_prev + u, u)
        a = jnp.where(valid, a * a_prev, a)
    h_out[...] = streams(u)[:, CHUNK - 1:CHUNK, :]
    y_scr[:, :, 0:R_WIDTH] = streams(_gelu_tanh(flat(z_scr[:, :, O_GB:O_GB + R_WIDTH])) * u)

    q = streams(rot(flat(z_scr[:, :, O_TQ:O_TQ + N_HEADS * DK])).astype(BF16))
    kf = streams(rot(flat(z_scr[:, :, O_TK:O_TK + N_HEADS * DK])) * (DK ** -0.5))
    for h in range(N_HEADS):
        hs = slice(h * DK, (h + 1) * DK)
        lg = lgs[h]
        dec = jnp.where(causal, jnp.exp(dist * lg), 0.0)
        w_inter = jnp.exp((tf + 1.0) * lg)
        w_key = jnp.exp((CHUNK - 1.0 - tf) * lg)
        s_decay = math.exp(CHUNK * lg)
        qh = q[:, :, hs]
        kh = kf[:, :, hs]
        vb = z_scr[:, :, O_TV + h * DV:O_TV + (h + 1) * DV].astype(BF16)
        sst = r_out[:, h]
        att = _bdot_nt(qh, kh.astype(BF16)) * dec
        o = _bdot(att.astype(BF16), vb) + w_inter * _bdot(qh, sst.astype(BF16))
        r_out[:, h] = s_decay * sst + _bdot_tn((kh * w_key).astype(BF16), vb)
        tg = z_scr[:, :, O_TG + h * DV:O_TG + (h + 1) * DV]
        y_scr[:, :, HALF + h * DV:HALF + (h + 1) * DV] = _silu(tg) * _head_norm(o, gt_ref[:, h * DV:(h + 1) * DV])

    _mixer_tail(x_scr, y_scr, wout_ref, ln_ref, wr_ref, br_ref, lt_ref, ut_ref,
                x1_ref, xs_ref, gs_ref, dsl_ref, meta_ref)


def _moe_kernel(nused_ref, ea_ref, eb_ref, wid_ref, xs_hbm, gs_hbm, wga_ref, wua_ref, wda_ref, wgb_ref, wub_ref,
                wdb_ref, ys_hbm, xbuf, gbuf, ybuf, sem_x, sem_g, sem_y):
    i = pl.program_id(0)
    n_used = nused_ref[0]
    slot = i % 2

    def x_copy(w, sl, k):
        return pltpu.make_async_copy(xs_hbm.at[w], xbuf.at[sl, k], sem_x.at[sl, k])

    def g_copy(w, sl, k):
        return pltpu.make_async_copy(gs_hbm.at[w], gbuf.at[sl, k], sem_g.at[sl, k])

    def y_copy(w, sl, k):
        return pltpu.make_async_copy(ybuf.at[sl, k], ys_hbm.at[w], sem_y.at[sl, k])

    def for_windows(tile, fn):
        for k in range(WPT):
            w = wid_ref[tile * WPT + k]

            @pl.when(w >= 0)
            def _():
                fn(w, k)

    def start_gather(tile, sl):
        def fn(w, k):
            x_copy(w, sl, k).start()
            g_copy(w, sl, k).start()
        for_windows(tile, fn)

    @pl.when(i == 0)
    def _():
        xbuf[...] = jnp.zeros(xbuf.shape, xbuf.dtype)
        gbuf[...] = jnp.zeros(gbuf.shape, gbuf.dtype)
        start_gather(0, 0)

    @pl.when(i + 1 < n_used)
    def _():
        start_gather(i + 1, 1 - slot)

    @pl.when(i < n_used)
    def _():
        def wait_in(w, k):
            x_copy(w, slot, k).wait()
            g_copy(w, slot, k).wait()
        for_windows(i, wait_in)

        @pl.when(i >= 2)
        def _():
            for_windows(i - 2, lambda w, k: y_copy(w, slot, k).wait())

        x = xbuf[slot].reshape(TILE_ROWS, D_MODEL)
        g = gbuf[slot].reshape(TILE_ROWS, LANE)
        g_a = g[:, 0:1] + g[:, 1:2]
        g_b = g[:, 2:3] + g[:, 3:4]
        h_a = _silu(_dot(x, wga_ref[0])) * _dot(x, wua_ref[0]) * g_a
        h_b = _silu(_dot(x, wgb_ref[0])) * _dot(x, wub_ref[0]) * g_b
        y = _dot(h_a.astype(BF16), wda_ref[0]) + _dot(h_b.astype(BF16), wdb_ref[0])
        ybuf[slot] = y.astype(BF16).reshape(WPT, WIN, D_MODEL)
        for_windows(i, lambda w, k: y_copy(w, slot, k).start())

        @pl.when(i == n_used - 1)
        def _():
            for_windows(i, lambda w, k: y_copy(w, slot, k).wait())

            @pl.when(i >= 1)
            def _():
                for_windows(i - 1, lambda w, k: y_copy(w, 1 - slot, k).wait())


def _final_kernel(x1_ref, ys_ref, dsl_ref, ln_ref, yp_ref, ysm_ref):
    j = pl.program_id(0)
    x2 = _combine_ln(x1_ref, ys_ref, dsl_ref, ln_ref).reshape(SEQ_BLK, CHUNK, D_MODEL)

    @pl.when(j < N_CHUNK)
    def _():
        yp_ref[...] = x2

    @pl.when(j >= N_CHUNK)
    def _():
        ysm_ref[...] = x2


def _const_spec(shape):
    nd = len(shape)
    return pl.BlockSpec(shape, lambda j: (0,) * nd, pipeline_mode=pl.Buffered(1))


def _row_spec(rows, width):
    return pl.BlockSpec((rows, width), lambda j: (j, 0))


def _prompt_spec():
    return pl.BlockSpec((SEQ_BLK, CHUNK, D_MODEL), lambda j: (0, jnp.minimum(j, N_CHUNK - 1), 0))


def _sample_spec():
    return pl.BlockSpec((SEQ_BLK, CHUNK, D_MODEL), lambda j: (jnp.maximum(j - N_CHUNK, 0), 0, 0))


def _state_in_spec(shape):
    nd = len(shape)
    return pl.BlockSpec((SEQ_BLK,) + shape[1:], lambda j: (jnp.maximum(j - N_CHUNK, 0),) + (0,) * (nd - 1))


def _state_out_spec(shape):
    nd = len(shape)
    return pl.BlockSpec((SEQ_BLK,) + shape[1:], lambda j: (jnp.maximum(j - N_CHUNK + 1, 0),) + (0,) * (nd - 1))


def _head_specs(first):
    if first:
        return [_prompt_spec(), _sample_spec()]
    return [_row_spec(ROW_BLK, D_MODEL), _row_spec(CAP_ROWS, D_MODEL), _row_spec(ROW_BLK, LANE),
            _const_spec((2, D_MODEL))]


def _mixer_call(kernel_fn, name, first, head, consts, states, width, extra_scratch=()):
    const_specs = [c[1] if isinstance(c, tuple) else _const_spec(c.shape) for c in consts]
    consts = [c[0] if isinstance(c, tuple) else c for c in consts]
    state_specs = [_state_in_spec(s.shape) for s in states]
    out_shapes = [jax.ShapeDtypeStruct((N_TOK, D_MODEL), F32),
                  jax.ShapeDtypeStruct((N_STEPS * CAP_ROWS, D_MODEL), BF16),
                  jax.ShapeDtypeStruct((N_STEPS * CAP_ROWS, LANE), F32),
                  jax.ShapeDtypeStruct((N_TOK, LANE), F32),
                  jax.ShapeDtypeStruct((N_STEPS * 8, LANE), F32)]
    out_specs = [_row_spec(ROW_BLK, D_MODEL), _row_spec(CAP_ROWS, D_MODEL), _row_spec(CAP_ROWS, LANE),
                 _row_spec(ROW_BLK, LANE), _row_spec(8, LANE)]
    for s in states:
        shp = (N_STATE_SEQ,) + s.shape[1:]
        out_shapes.append(jax.ShapeDtypeStruct(shp, F32))
        out_specs.append(_state_out_spec(shp))
    return pl.pallas_call(
        functools.partial(kernel_fn, first=first),
        grid=(N_STEPS,),
        in_specs=_head_specs(first) + const_specs + state_specs,
        out_specs=out_specs,
        out_shape=out_shapes,
        scratch_shapes=[pltpu.VMEM((ROW_BLK, D_MODEL), F32), pltpu.VMEM((SEQ_BLK, CHUNK, width), F32),
                        pltpu.VMEM((SEQ_BLK, CHUNK, D_MODEL), F32)] + list(extra_scratch),
        compiler_params=pltpu.CompilerParams(dimension_semantics=("arbitrary",), vmem_limit_bytes=VMEM_LIMIT),
        name=name,
    )(*head, *consts, *states)


def _moe_call(n_used, tile_ea, tile_eb, wid, xs, gs, wg, wu, wd, layer):
    xs_w = xs.reshape(N_WIN, WIN, D_MODEL)
    gs_w = gs.reshape(N_WIN, WIN, LANE)
    wspec = lambda shape, sel: pl.BlockSpec((None, 1) + shape,
                                            lambda i, nu, ea, eb, wi: (layer, (ea, eb)[sel][i], 0, 0))
    grid_spec = pltpu.PrefetchScalarGridSpec(
        num_scalar_prefetch=4,
        grid=(N_TILES,),
        in_specs=[pl.BlockSpec(memory_space=pl.ANY), pl.BlockSpec(memory_space=pl.ANY),
                  wspec((D_MODEL, D_EXPERT), 0), wspec((D_MODEL, D_EXPERT), 0), wspec((D_EXPERT, D_MODEL), 0),
                  wspec((D_MODEL, D_EXPERT), 1), wspec((D_MODEL, D_EXPERT), 1), wspec((D_EXPERT, D_MODEL), 1)],
        out_specs=pl.BlockSpec(memory_space=pl.ANY),
        scratch_shapes=[pltpu.VMEM((2, WPT, WIN, D_MODEL), BF16), pltpu.VMEM((2, WPT, WIN, LANE), F32),
                        pltpu.VMEM((2, WPT, WIN, D_MODEL), BF16),
                        pltpu.SemaphoreType.DMA((2, WPT)), pltpu.SemaphoreType.DMA((2, WPT)),
                        pltpu.SemaphoreType.DMA((2, WPT))])
    ys_w = pl.pallas_call(
        _moe_kernel,
        grid_spec=grid_spec,
        out_shape=jax.ShapeDtypeStruct((N_WIN, WIN, D_MODEL), BF16),
        input_output_aliases={4: 0},
        compiler_params=pltpu.CompilerParams(dimension_semantics=("arbitrary",), vmem_limit_bytes=VMEM_LIMIT),
        name="moe",
    )(n_used, tile_ea, tile_eb, wid, xs_w, gs_w, wg, wu, wd, wg, wu, wd)
    return ys_w.reshape(N_STEPS * CAP_ROWS, D_MODEL)


def _final_call(x1, ys, dsl, ln):
    return pl.pallas_call(
        _final_kernel,
        grid=(N_STEPS,),
        in_specs=[_row_spec(ROW_BLK, D_MODEL), _row_spec(CAP_ROWS, D_MODEL), _row_spec(ROW_BLK, LANE),
                  _const_spec((2, D_MODEL))],
        out_specs=[_prompt_spec(), _sample_spec()],
        out_shape=[jax.ShapeDtypeStruct((BATCH, SEQ, D_MODEL), F32),
                   jax.ShapeDtypeStruct((DEC_BATCH, DEC_SEQ, D_MODEL), F32)],
        compiler_params=pltpu.CompilerParams(dimension_semantics=("arbitrary",), vmem_limit_bytes=VMEM_LIMIT),
        name="final_ln",
    )(x1, ys, dsl, ln)


def _plan_kernel(nw_ref, wo_ref, nused_ref, ea_ref, eb_ref, wid_ref):
    n_pairs = len(PAIRS)

    def class_body(c, pos):
        tile0 = pos // WPT

        def block_body(b, pos):
            n = nw_ref[b * N_CLASS + c]
            base = b * CAP_WIN + wo_ref[b * N_CLASS + c] - pos

            def win_body(p, carry):
                wid_ref[p] = base + p
                return carry
            lax.fori_loop(pos, pos + n, win_body, 0)
            return pos + n
        pos = lax.fori_loop(0, N_STEPS, block_body, pos)
        padded = (pos + (WPT - 1)) // WPT * WPT

        def pad_body(p, carry):
            wid_ref[p] = -1
            return carry
        lax.fori_loop(pos, padded, pad_body, 0)
        grp = c // n_pairs
        p = c - grp * n_pairs
        ge3 = jnp.where(p >= 3, 1, 0)
        ge5 = jnp.where(p >= 5, 1, 0)
        e_a = grp * EXPERTS_PER_GROUP + ge3 + ge5
        e_b = grp * EXPERTS_PER_GROUP + p + 1 - 2 * ge3 - ge5

        def tile_body(t, carry):
            ea_ref[t] = e_a
            eb_ref[t] = e_b
            return carry
        lax.fori_loop(tile0, padded // WPT, tile_body, 0)
        return padded

    end = lax.fori_loop(0, N_CLASS, class_body, 0)
    n_used = end // WPT
    nused_ref[0] = n_used
    last = jnp.maximum(n_used - 1, 0)
    last_a = ea_ref[last]
    last_b = eb_ref[last]

    def tail_tile(t, carry):
        ea_ref[t] = last_a
        eb_ref[t] = last_b
        return carry
    lax.fori_loop(n_used, N_TILES, tail_tile, 0)

    def tail_wid(p, carry):
        wid_ref[p] = -1
        return carry
    lax.fori_loop(end, N_TILES * WPT, tail_wid, 0)


def _plan_call(meta):
    i32 = jnp.int32
    meta = meta.reshape(N_STEPS, 8, LANE)
    nw = meta[:, 1, :N_CLASS].astype(i32).reshape(-1)
    wo = meta[:, 2, :N_CLASS].astype(i32).reshape(-1)
    smem = pl.BlockSpec(memory_space=pltpu.SMEM)
    return pl.pallas_call(
        _plan_kernel,
        in_specs=[smem, smem],
        out_specs=[smem, smem, smem, smem],
        out_shape=[jax.ShapeDtypeStruct((1,), i32), jax.ShapeDtypeStruct((N_TILES,), i32),
                   jax.ShapeDtypeStruct((N_TILES,), i32), jax.ShapeDtypeStruct((N_TILES * WPT,), i32)],
        name="moe_plan",
    )(nw, wo)


def _gla_sum_matrix():
    t = np.arange(CHUNK)
    mats = [(t[None, :] <= t[:, None])]
    for m in GLA_LEVELS:
        mid = (t // (2 * m)) * (2 * m) + m
        upper = (t % (2 * m)) >= m
        up = upper[:, None] & (t[None, :] > mid[:, None]) & (t[None, :] <= t[:, None])
        lo = (~upper)[:, None] & (t[None, :] > t[:, None]) & (t[None, :] <= mid[:, None])
        mats.append(up | lo)
    mat = np.concatenate(mats, axis=0).astype(np.float32)
    return jnp.asarray(np.concatenate([mat, mat, mat], axis=1), dtype=BF16)


def _rope_table():
    inv = ROPE_BASE ** (-jnp.arange(0, DK, 2, dtype=F32) / DK)
    pos = jnp.concatenate([jnp.arange(SEQ, dtype=F32)] + [PAST_LEN + jnp.arange(DEC_SEQ, dtype=F32)] * N_SAMPLE_BLK)
    ang = pos[:, None] * inv[None]
    cos, sin = jnp.cos(ang), jnp.sin(ang)
    cosf = jnp.tile(jnp.concatenate([cos, cos], axis=1), (1, N_HEADS))
    sinf = jnp.tile(jnp.concatenate([-sin, sin], axis=1), (1, N_HEADS))
    return jnp.concatenate([cosf, sinf], axis=1)


def _pad_lanes(a, width=LANE):
    return jnp.pad(a, [(0, 0)] * (a.ndim - 1) + [(0, width - a.shape[-1])])


def _router_pack(w_grp, b_grp, w_rt, b_rt):
    w = jnp.zeros((D_MODEL, LANE), F32).at[:, 0:N_GROUPS].set(w_grp).at[:, ROUTE_E0:ROUTE_E0 + N_EXPERTS].set(w_rt)
    wh = w.astype(BF16)
    wl = (w - wh.astype(F32)).astype(BF16)
    b = jnp.zeros((1, LANE), F32).at[0, 0:N_GROUPS].set(b_grp).at[0, ROUTE_E0:ROUTE_E0 + N_EXPERTS].set(b_rt)
    return jnp.concatenate([wh, wl], axis=1), b


def kernel(x_prompt, x_sample, state_mlstm_c, state_mlstm_n, state_mlstm_m, state_gla, state_conv, state_rglru, state_ret, w_in_even, b_mlstm_i, b_mlstm_f, g_mlstm_norm, w_gla_a2, b_gla_a, g_gla_norm, w_out_even, w_in_odd, w_conv, b_conv, w_lru_r, b_lru_r, w_lru_i, b_lru_i, lru_lambda, g_ret_norm, w_out_odd, ln1_g, ln1_b, ln2_g, ln2_b, w_router_group, b_router_group, w_router_expert, b_router_expert, w_exp_gate, w_exp_up, w_exp_down):
    cm = _gla_sum_matrix()
    rope = _rope_table()
    tt = np.arange(ROW_BLK)
    lower_strict = jnp.asarray((tt[None, :] < tt[:, None]).astype(np.float32), dtype=BF16)
    ll = np.arange(LANE)
    upper_strict = jnp.asarray((ll[:, None] < ll[None, :]).astype(np.float32), dtype=BF16)

    wg_all, wu_all, wd_all = w_exp_gate.astype(BF16), w_exp_up.astype(BF16), w_exp_down.astype(BF16)
    head = [x_prompt.astype(F32), x_sample.astype(F32)]
    new = {k: [] for k in ("c", "n", "m", "g", "conv", "h", "r")}
    for layer in range(DEPTH):
        li = layer // 2
        first = layer == 0
        wr, br = _router_pack(w_router_group[layer], b_router_group[layer], w_router_expert[layer],
                              b_router_expert[layer])
        ln1 = jnp.stack([ln1_g[layer], ln1_b[layer]])
        common = [ln1, wr, br, lower_strict, upper_strict]
        if layer % 2 == 0:
            w = w_in_even[li]
            win = jnp.concatenate([w[:, 0:1024], w[:, 1032:1544], w[:, 1544:2568], w[:, 2584:3096], w[:, 1024:1032],
                                   w[:, 2568:2584], jnp.zeros((D_MODEL, EVEN_W - 3096), w.dtype)], axis=1).astype(BF16)
            bsm = jnp.zeros((1, LANE), F32).at[0, 0:N_HEADS].set(b_mlstm_i[li]).at[0, N_HEADS:2 * N_HEADS].set(b_mlstm_f[li])
            wa2 = jnp.zeros((LANE, N_HEADS * DK), F32).at[2 * N_HEADS:2 * N_HEADS + G_RANK].set(w_gla_a2[li]).astype(BF16)
            consts = [win, w_out_even[li].astype(BF16)] + common + [cm, bsm, g_mlstm_norm[li][None, :], wa2,
                                                                     b_gla_a[li][None, :], g_gla_norm[li][None, :]]
            states = [state_mlstm_c[li].astype(F32), state_mlstm_n[li].astype(F32).reshape(DEC_BATCH, 1, N_HEADS * DK),
                      _pad_lanes(state_mlstm_m[li].astype(F32))[:, None, :], state_gla[li].astype(F32)]
            x1, xs, gs, dsl, meta, c, n, m, g = _mixer_call(_even_kernel, "even_mixer", first, head, consts, states,
                                                            EVEN_W)
            new["c"].append(c)
            new["n"].append(n.reshape(N_STATE_SEQ, N_HEADS, DK))
            new["m"].append(m[:, 0, 0:N_HEADS])
            new["g"].append(g)
        else:
            cw = jnp.concatenate([w_conv[li], b_conv[li][None, :]], axis=0)
            wr4 = w_lru_r[li].reshape(2, 4, R_BS, R_BS)
            wi4 = w_lru_i[li].reshape(2, 4, R_BS, R_BS)
            eye4 = jnp.eye(4, dtype=F32)
            bd = lambda w4: jnp.einsum("tncd,nm->tncmd", w4, eye4).reshape(2, 256, 256)
            wri = jnp.concatenate([bd(wr4), bd(wi4)], axis=2).astype(BF16)
            bri = jnp.stack([b_lru_r[li], b_lru_i[li]])
            consts = [w_in_odd[li].astype(BF16), w_out_odd[li].astype(BF16)] + common + [
                (rope, pl.BlockSpec((CHUNK, 2 * N_HEADS * DK), lambda j: (j, 0))), cw, wri, bri,
                lru_lambda[li][None, :], g_ret_norm[li][None, :]]
            states = [state_conv[li].astype(F32), state_rglru[li].astype(F32)[:, None, :], state_ret[li].astype(F32)]
            x1, xs, gs, dsl, meta, cv, hh, r = _mixer_call(_odd_kernel, "odd_mixer", first, head, consts, states, ODD_W,
                                                           extra_scratch=[pltpu.VMEM((SEQ_BLK, 8 + CHUNK, R_WIDTH), F32)])
            new["conv"].append(cv)
            new["h"].append(hh[:, 0, :])
            new["r"].append(r)
        n_used, tile_ea, tile_eb, wid = _plan_call(meta)
        ys = _moe_call(n_used, tile_ea, tile_eb, wid, xs, gs, wg_all, wu_all, wd_all, layer)
        head = [x1, ys, dsl, jnp.stack([ln2_g[layer], ln2_b[layer]])]

    y_prompt, y_sample = _final_call(*head)
    st = {k: jnp.stack(v) for k, v in new.items()}
    order = ("c", "n", "m", "g", "conv", "h", "r")
    dtypes = (state_mlstm_c.dtype, state_mlstm_n.dtype, state_mlstm_m.dtype, state_gla.dtype, state_conv.dtype,
              state_rglru.dtype, state_ret.dtype)
    prompt_states = tuple(st[k][:, :BATCH].astype(d) for k, d in zip(order, dtypes))
    sample_states = tuple(st[k][:, BATCH:].astype(d) for k, d in zip(order, dtypes))
    return (y_prompt.astype(x_prompt.dtype), y_sample.astype(x_sample.dtype)) + prompt_states + sample_states
```
